```python
import math
import jax, jax.numpy as jnp
from jax import lax
import numpy as np

D_MODEL = 1024
BATCH = 2
SEQ = 8192
DEPTH = 4

CHUNK = 64
N_META = 16
N_A = DEPTH // 2
N_B = DEPTH - N_A
D_FF = 2816
POOL_WINDOWS = (2, 4, 8, 16)
N_POOL_GROUPS = len(POOL_WINDOWS)
POOL_GROUP = D_MODEL // N_POOL_GROUPS
N_HEADS = 8
QK_NOPE = 64
QK_ROPE = 32
V_HEAD = 64
KV_RANK = 256
Q_RANK = 384
ROPE_THETA = 10000.0
Q_BLOCK = 128
EPS = 1e-6

kernel_name = "yoco_pool_mla_macaron_trunk"


def rmsnorm(x, g):
    xf = x.astype(jnp.float32)
    xf = xf * lax.rsqrt(jnp.mean(xf * xf, axis=-1, keepdims=True) + EPS)
    return xf.astype(x.dtype) * g


def swiglu(h, w_gate, w_up, w_down):
    return (jax.nn.silu(h @ w_gate) * (h @ w_up)) @ w_down


def chunk_ids(n):
    pos = jnp.arange(n)
    return jnp.where(pos < N_META, 0, (pos - N_META) // CHUNK + 1)


def rope_tables(n):
    inv = 1.0 / (ROPE_THETA ** (jnp.arange(0, QK_ROPE, 2, dtype=jnp.float32) / QK_ROPE))
    ang = jnp.arange(n, dtype=jnp.float32)[:, None] * inv[None, :]
    return jnp.cos(ang), jnp.sin(ang)


def apply_rope(x, cos, sin):
    xf = x.astype(jnp.float32)
    x1, x2 = xf[..., : QK_ROPE // 2], xf[..., QK_ROPE // 2:]
    out = jnp.concatenate([x1 * cos - x2 * sin, x2 * cos + x1 * sin], axis=-1)
    return out.astype(x.dtype)


def pool_mixer(h, w_group, scale):
    L = h.shape[1]
    hf = h.astype(jnp.float32)
    cs = jnp.concatenate([jnp.zeros_like(hf[:, :1]), jnp.cumsum(hf, axis=1)], axis=1)
    hi = jnp.arange(1, L + 1)
    outs = []
    for g, w in enumerate(POOL_WINDOWS):
        sl = slice(g * POOL_GROUP, (g + 1) * POOL_GROUP)
        lo = jnp.maximum(hi - w, 0)
        c = cs[..., sl]
        count = (hi - lo).astype(jnp.float32)[None, :, None]
        mean = (jnp.take(c, hi, axis=1) - jnp.take(c, lo, axis=1)) / count
        outs.append(mean - hf[..., sl])
    pooled = jnp.stack(outs, axis=2).astype(h.dtype)
    y = jnp.einsum('blgc,gcd->blgd', pooled, w_group)
    return y.reshape(h.shape) * scale


def mla_shared_kv(h, w_dkv, kv_latent_norm, w_uk, w_uv, cos, sin):
    B, L, _ = h.shape
    ckr = h @ w_dkv
    c_kv = rmsnorm(ckr[..., :KV_RANK], kv_latent_norm)
    k_rope = apply_rope(ckr[..., KV_RANK:], cos, sin)
    k_nope = (c_kv @ w_uk).reshape(B, L, N_HEADS, QK_NOPE)
    v = (c_kv @ w_uv).reshape(B, L, N_HEADS, V_HEAD)
    return k_nope, k_rope, v


def mla_attention(h, w_dq, q_latent_norm, w_uq, w_o, k_nope, k_rope, v, cos, sin):
    B, L, _ = h.shape
    cq = rmsnorm(h @ w_dq, q_latent_norm)
    q = (cq @ w_uq).reshape(B, L, N_HEADS, QK_NOPE + QK_ROPE)
    q_nope = q[..., :QK_NOPE]
    q_rope = apply_rope(q[..., QK_NOPE:], cos[:, None, :], sin[:, None, :])
    n_blk = -(-L // Q_BLOCK)
    Lp = n_blk * Q_BLOCK
    pad = ((0, 0), (0, Lp - L), (0, 0), (0, 0))
    qn_b = jnp.pad(q_nope, pad).reshape(B, n_blk, Q_BLOCK, N_HEADS, QK_NOPE).transpose(1, 0, 2, 3, 4)
    qr_b = jnp.pad(q_rope, pad).reshape(B, n_blk, Q_BLOCK, N_HEADS, QK_ROPE).transpose(1, 0, 2, 3, 4)
    qid_b = chunk_ids(Lp).reshape(n_blk, Q_BLOCK)
    kid = chunk_ids(L)
    sm_scale = 1.0 / math.sqrt(QK_NOPE + QK_ROPE)

    def block(args):
        qn, qr, qid = args
        s = (jnp.einsum('bqhd,bkhd->bhqk', qn, k_nope)
             + jnp.einsum('bqhr,bkr->bhqk', qr, k_rope)).astype(jnp.float32) * sm_scale
        mask = kid[None, :] <= qid[:, None]
        s = jnp.where(mask[None, None], s, jnp.finfo(jnp.float32).min)
        p = jax.nn.softmax(s, axis=-1).astype(v.dtype)
        return jnp.einsum('bhqk,bkhd->bqhd', p, v)

    o = lax.map(block, (qn_b, qr_b, qid_b))
    o = o.transpose(1, 0, 2, 3, 4).reshape(B, Lp, N_HEADS * V_HEAD)[:, :L]
    return o @ w_o


def setup_inputs(seed: int = 0) -> dict:
    key = jax.random.key(seed)
    ks = iter(jax.random.split(key, 40))
    f32 = jnp.float32

    def nrm(shape, fan_in):
        return jax.random.normal(next(ks), shape, f32) * (fan_in ** -0.5)

    def gain(shape):
        return 1.0 + 0.1 * jax.random.normal(next(ks), shape, f32)

    return {
        "x": jax.random.normal(next(ks), (BATCH, SEQ, D_MODEL), f32),
        "meta_tokens": jax.random.normal(next(ks), (N_META, D_MODEL), f32),
        "ffn1_norm": gain((DEPTH, D_MODEL)),
        "ffn1_w_gate": nrm((DEPTH, D_MODEL, D_FF), D_MODEL),
        "ffn1_w_up": nrm((DEPTH, D_MODEL, D_FF), D_MODEL),
        "ffn1_w_down": nrm((DEPTH, D_FF, D_MODEL), D_FF),
        "mix_norm": gain((DEPTH, D_MODEL)),
        "ffn2_norm": gain((DEPTH, D_MODEL)),
        "ffn2_w_gate": nrm((DEPTH, D_MODEL, D_FF), D_MODEL),
        "ffn2_w_up": nrm((DEPTH, D_MODEL, D_FF), D_MODEL),
        "ffn2_w_down": nrm((DEPTH, D_FF, D_MODEL), D_FF),
        "pool_w": nrm((N_A, N_POOL_GROUPS, POOL_GROUP, POOL_GROUP), POOL_GROUP),
        "pool_scale": gain((N_A, D_MODEL)),
        "kv_in_norm": gain((D_MODEL,)),
        "w_dkv": nrm((D_MODEL, KV_RANK + QK_ROPE), D_MODEL),
        "kv_latent_norm": gain((KV_RANK,)),
        "w_uk": nrm((KV_RANK, N_HEADS * QK_NOPE), KV_RANK),
        "w_uv": nrm((KV_RANK, N_HEADS * V_HEAD), KV_RANK),
        "w_dq": nrm((N_B, D_MODEL, Q_RANK), D_MODEL),
        "q_latent_norm": gain((N_B, Q_RANK)),
        "w_uq": nrm((N_B, Q_RANK, N_HEADS * (QK_NOPE + QK_ROPE)), Q_RANK),
        "w_o": nrm((N_B, N_HEADS * V_HEAD, D_MODEL), N_HEADS * V_HEAD),
        "final_norm": gain((D_MODEL,)),
    }


def reference(x, meta_tokens, ffn1_norm, ffn1_w_gate, ffn1_w_up, ffn1_w_down, mix_norm,
              ffn2_norm, ffn2_w_gate, ffn2_w_up, ffn2_w_down, pool_w, pool_scale,
              kv_in_norm, w_dkv, kv_latent_norm, w_uk, w_uv, w_dq, q_latent_norm, w_uq, w_o,
              final_norm):
    B = x.shape[0]
    meta = jnp.broadcast_to(meta_tokens[None].astype(x.dtype), (B, N_META, D_MODEL))
    h = jnp.concatenate([meta, x], axis=1)
    L = h.shape[1]
    cos, sin = rope_tables(L)
    shared = None
    for l in range(DEPTH):
        h = h + 0.5 * swiglu(rmsnorm(h, ffn1_norm[l]), ffn1_w_gate[l], ffn1_w_up[l], ffn1_w_down[l])
        u = rmsnorm(h, mix_norm[l])
        if l < N_A:
            h = h + pool_mixer(u, pool_w[l], pool_scale[l])
        else:
            j = l - N_A
            k_nope, k_rope, v = shared
            h = h + mla_attention(u, w_dq[j], q_latent_norm[j], w_uq[j], w_o[j],
                                  k_nope, k_rope, v, cos, sin)
        h = h + 0.5 * swiglu(rmsnorm(h, ffn2_norm[l]), ffn2_w_gate[l], ffn2_w_up[l], ffn2_w_down[l])
        if l == N_A - 1:
            shared = mla_shared_kv(rmsnorm(h, kv_in_norm), w_dkv, kv_latent_norm, w_uk, w_uv, cos, sin)
    out = rmsnorm(h, final_norm)
    return out[:, N_META:]
```

```python
import functools
import math

import jax
import jax.numpy as jnp
from jax import lax
from jax.experimental import pallas as pl
from jax.experimental.pallas import tpu as pltpu

D = 1024
B = 2
SEQ = 8192
DEPTH = 4
CHUNK = 64
N_META = 16
N_A = DEPTH // 2
D_FF = 2816
POOL_WINDOWS = (2, 4, 8, 16)
POOL_GROUP = D // len(POOL_WINDOWS)
H = 8
QK_NOPE = 64
QK_ROPE = 32
V_HEAD = 64
KV_RANK = 256
Q_RANK = 384
ROPE_THETA = 10000.0
EPS = 1e-6

F32 = jnp.float32
BF16 = jnp.bfloat16

LANES = 128
TM = 512
NT_REAL = B * SEQ // TM
NT = NT_REAL + 1
ROWS = NT * TM
TILES_PER_SEQ = SEQ // TM
META_ROW0 = B * SEQ
FC = 256
KB = 256
BQ = TM
NS = BQ // KB
HALO = 16
NEG = -1e30
VMEM_LIMIT = 56 * 1024 * 1024


def _rms(x, g):
    ms = jnp.mean(x * x, axis=-1, keepdims=True)
    return (x * lax.rsqrt(ms + EPS)) * g


def _params(n_axes=1):
    return pltpu.CompilerParams(
        dimension_semantics=("arbitrary",) * n_axes, vmem_limit_bytes=VMEM_LIMIT)


def _const_spec(shape, n_axes=1):
    zeros = (0,) * len(shape)
    if n_axes == 1:
        return pl.BlockSpec(shape, lambda i: zeros, pipeline_mode=pl.Buffered(1))
    return pl.BlockSpec(shape, lambda i, j: zeros, pipeline_mode=pl.Buffered(1))


def _ffn_kernel(x_ref, g_ref, wg_ref, wu_ref, wd_ref, *rest, final):
    if final:
        gf_ref, o_ref, xn_ref, a_ref = rest
    else:
        o_ref, xn_ref, a_ref = rest
    x = x_ref[...]
    xn_ref[...] = _rms(x, g_ref[...]).astype(BF16)
    for c in range(D_FF // FC):
        sl = slice(c * FC, (c + 1) * FC)
        gate = jnp.dot(xn_ref[...], wg_ref[:, sl], preferred_element_type=F32)
        up = jnp.dot(xn_ref[...], wu_ref[:, sl], preferred_element_type=F32)
        a_ref[:, sl] = (gate * jax.nn.sigmoid(gate) * up).astype(BF16)
    y = x + 0.5 * jnp.dot(a_ref[...], wd_ref[...], preferred_element_type=F32)
    if final:
        y = _rms(y, gf_ref[...])
    o_ref[...] = y


def _ffn(h, g, wg, wu, wd, final_g=None):
    final = final_g is not None
    n_tiles = NT_REAL if final else NT
    row_spec = pl.BlockSpec((TM, D), lambda i: (i, 0))
    in_specs = [row_spec, _const_spec((1, D)), _const_spec((D, D_FF)),
                _const_spec((D, D_FF)), _const_spec((D_FF, D))]
    args = [h, g.reshape(1, D), wg, wu, wd]
    if final:
        in_specs.append(_const_spec((1, D)))
        args.append(final_g.reshape(1, D))
    return pl.pallas_call(
        functools.partial(_ffn_kernel, final=final),
        grid=(n_tiles,),
        in_specs=in_specs,
        out_specs=row_spec,
        out_shape=jax.ShapeDtypeStruct((n_tiles * TM, D), F32),
        scratch_shapes=[pltpu.VMEM((TM, D), BF16), pltpu.VMEM((TM, D_FF), BF16)],
        compiler_params=_params(),
        name="ffn_final" if final else "ffn",
    )(*args)


def _pool_kernel(x_ref, halo_ref, g_ref, w_ref, s_ref, o_ref):
    i = pl.program_id(0)
    is_meta = i == NT - 1
    x = x_ref[...]
    g = g_ref[...]
    u = _rms(x, g)
    uh = _rms(halo_ref[...], g) * jnp.where(is_meta, 0.0, 1.0)
    pos = lax.broadcasted_iota(jnp.int32, (TM, POOL_GROUP), 0) + jnp.where(is_meta, 0, N_META)
    pad = jnp.zeros((HALO, POOL_GROUP), F32)
    for gi, w in enumerate(POOL_WINDOWS):
        sl = slice(gi * POOL_GROUP, (gi + 1) * POOL_GROUP)
        ug = u[:, sl]
        acc = jnp.concatenate([pad, uh[:, sl], ug], axis=0)
        step = 1
        while step < w:
            acc = acc + pltpu.roll(acc, step, 0)
            step *= 2
        cnt = jnp.minimum(pos + 1, w).astype(F32)
        pooled = acc[2 * HALO:, :] / cnt - ug
        y = jnp.dot(pooled.astype(BF16), w_ref[gi], preferred_element_type=F32)
        o_ref[:, sl] = x[:, sl] + y * s_ref[:, sl]


def _pool(h, g, w, scale):
    def halo_map(i):
        first = (i % TILES_PER_SEQ) == 0
        return (jnp.where(first, META_ROW0 // HALO, i * (TM // HALO) - 1), 0)

    row_spec = pl.BlockSpec((TM, D), lambda i: (i, 0))
    return pl.pallas_call(
        _pool_kernel,
        grid=(NT,),
        in_specs=[row_spec, pl.BlockSpec((HALO, D), halo_map), _const_spec((1, D)),
                  _const_spec((len(POOL_WINDOWS), POOL_GROUP, POOL_GROUP)), _const_spec((1, D))],
        out_specs=row_spec,
        out_shape=jax.ShapeDtypeStruct((ROWS, D), F32),
        compiler_params=_params(),
        name="pool",
    )(h, h, g.reshape(1, D), w, scale.reshape(1, D))


def _kv_kernel(x_ref, g_ref, wd_ref, gl_ref, wuk_ref, wuvt_ref, cos_ref, sin_ref, k_ref, vt_ref):
    xn = _rms(x_ref[...], g_ref[...]).astype(BF16)
    ckr = jnp.dot(xn, wd_ref[...], preferred_element_type=F32)
    ckv = _rms(ckr[:, :KV_RANK], gl_ref[...]).astype(BF16)
    kr = (ckr[:, KV_RANK:KV_RANK + LANES] * cos_ref[...]
          + ckr[:, KV_RANK + LANES:] * sin_ref[...])
    kn = jnp.dot(ckv, wuk_ref[...], preferred_element_type=F32)
    for hh in range(H):
        k_ref[hh] = (kn[:, hh * LANES:(hh + 1) * LANES] + kr).astype(BF16)
    vt = lax.dot_general(wuvt_ref[...], ckv, (((1,), (1,)), ((), ())),
                         preferred_element_type=F32)
    vt_ref[...] = vt.astype(BF16)


def _kv(h, g, wd, gl, wuk, wuvt, cos_t, sin_t):
    row_spec = pl.BlockSpec((TM, D), lambda i: (i, 0))
    tab_spec = pl.BlockSpec((TM, LANES), lambda i: (i, 0))
    return pl.pallas_call(
        _kv_kernel,
        grid=(NT,),
        in_specs=[row_spec, _const_spec((1, D)), _const_spec((D, KV_RANK + 2 * LANES)),
                  _const_spec((1, KV_RANK)), _const_spec((KV_RANK, H * LANES)),
                  _const_spec((H * V_HEAD, KV_RANK)), tab_spec, tab_spec],
        out_specs=[pl.BlockSpec((H, TM, LANES), lambda i: (0, i, 0)),
                   pl.BlockSpec((H * V_HEAD, TM), lambda i: (0, i))],
        out_shape=[jax.ShapeDtypeStruct((H, ROWS, LANES), BF16),
                   jax.ShapeDtypeStruct((H * V_HEAD, ROWS), BF16)],
        compiler_params=_params(),
        name="shared_kv",
    )(h, g.reshape(1, D), wd, gl.reshape(1, KV_RANK), wuk, wuvt, cos_t, sin_t)


def _q_kernel(x_ref, g_ref, wdq_ref, gl_ref, wq_ref, wqr_ref, cos_ref, sin_ref, q_ref):
    u = _rms(x_ref[...], g_ref[...]).astype(BF16)
    cq = jnp.dot(u, wdq_ref[...], preferred_element_type=F32)
    cqn = _rms(cq, gl_ref[...]).astype(BF16)
    qa = jnp.dot(cqn, wq_ref[...], preferred_element_type=F32)
    qb = jnp.dot(cqn, wqr_ref[...], preferred_element_type=F32)
    cos = cos_ref[...]
    sin = sin_ref[...]
    scale = (1.0 / math.sqrt(QK_NOPE + QK_ROPE)) * math.log2(math.e)
    for hh in range(H):
        sl = slice(hh * LANES, (hh + 1) * LANES)
        q_ref[hh] = ((qa[:, sl] * cos + qb[:, sl] * sin) * scale).astype(BF16)


def _q(h, g, wdq, gl, wq, wqr, cos_t, sin_t):
    row_spec = pl.BlockSpec((TM, D), lambda i: (i, 0))
    tab_spec = pl.BlockSpec((TM, LANES), lambda i: (i, 0))
    return pl.pallas_call(
        _q_kernel,
        grid=(NT,),
        in_specs=[row_spec, _const_spec((1, D)), _const_spec((D, Q_RANK)),
                  _const_spec((1, Q_RANK)), _const_spec((Q_RANK, H * LANES)),
                  _const_spec((Q_RANK, H * LANES)), tab_spec, tab_spec],
        out_specs=pl.BlockSpec((H, TM, LANES), lambda i: (0, i, 0)),
        out_shape=jax.ShapeDtypeStruct((H, ROWS, LANES), BF16),
        compiler_params=_params(),
        name="q_proj",
    )(h, g.reshape(1, D), wdq, gl.reshape(1, Q_RANK), wq, wqr, cos_t, sin_t)


def _attn_kernel(q_ref, k_ref, vt_ref, km_ref, vmt_ref, o_ref, m_ref, l_ref, acc_ref):
    j = pl.program_id(1)
    is_real = j < NT_REAL
    nt_dims = (((1,), (1,)), ((), ()))

    s = lax.dot_general(km_ref[0], q_ref[0], nt_dims, preferred_element_type=F32)
    kidx = lax.broadcasted_iota(jnp.int32, s.shape, 0)
    s = jnp.where(kidx < N_META, s, NEG)
    m0 = jnp.max(s, axis=0, keepdims=True)
    p = jnp.exp2(s - m0)
    m_ref[...] = m0
    l_ref[...] = jnp.sum(p, axis=0, keepdims=True)
    acc_ref[...] = jnp.dot(vmt_ref[...], p.astype(BF16), preferred_element_type=F32)

    def update(kblk, vtblk, c, mask=None):
        cs = slice(c * KB, (c + 1) * KB)
        s = lax.dot_general(kblk, q_ref[0, cs, :], nt_dims, preferred_element_type=F32)
        if mask is not None:
            s = jnp.where(mask, s, NEG)
        m_old = m_ref[:, cs]
        m_new = jnp.maximum(m_old, jnp.max(s, axis=0, keepdims=True))
        alpha = jnp.exp2(m_old - m_new)
        p = jnp.exp2(s - m_new)
        l_ref[:, cs] = alpha * l_ref[:, cs] + jnp.sum(p, axis=0, keepdims=True)
        acc_ref[:, cs] = alpha * acc_ref[:, cs] + jnp.dot(
            vtblk, p.astype(BF16), preferred_element_type=F32)
        m_ref[:, cs] = m_new

    def load(kb):
        off = pl.multiple_of(kb * KB, KB)
        return k_ref[0, pl.ds(off, KB), :], vt_ref[:, pl.ds(off, KB)]

    first_diag = (j % TILES_PER_SEQ) * NS
    n_full = jnp.where(is_real, first_diag, 0)

    def full_body(kb, carry):
        kblk, vtblk = load(kb)
        for c in range(NS):
            update(kblk, vtblk, c)
        return carry

    lax.fori_loop(0, n_full, full_body, 0)

    @pl.when(is_real)
    def _():
        kpos = lax.broadcasted_iota(jnp.int32, (KB, KB), 0)
        qpos = lax.broadcasted_iota(jnp.int32, (KB, KB), 1)
        chunk_mask = (kpos // CHUNK) <= (qpos // CHUNK)
        for d in range(NS):
            kblk, vtblk = load(first_diag + d)
            update(kblk, vtblk, d, chunk_mask)
            for c in range(d + 1, NS):
                update(kblk, vtblk, c)

    o_ref[...] = (acc_ref[...] / l_ref[...]).astype(BF16)


def _attn(q_all, k_all, vt_all):
    def b_of(j):
        return jnp.minimum(j // TILES_PER_SEQ, B - 1)

    return pl.pallas_call(
        _attn_kernel,
        grid=(H, NT),
        in_specs=[pl.BlockSpec((1, BQ, LANES), lambda h, j: (h, j, 0)),
                  pl.BlockSpec((1, SEQ, LANES), lambda h, j: (h, b_of(j), 0)),
                  pl.BlockSpec((V_HEAD, SEQ), lambda h, j: (h, b_of(j))),
                  pl.BlockSpec((1, LANES, LANES), lambda h, j: (h, META_ROW0 // LANES, 0)),
                  pl.BlockSpec((V_HEAD, LANES), lambda h, j: (h, META_ROW0 // LANES))],
        out_specs=pl.BlockSpec((V_HEAD, BQ), lambda h, j: (h, j)),
        out_shape=jax.ShapeDtypeStruct((H * V_HEAD, ROWS), BF16),
        scratch_shapes=[pltpu.VMEM((1, BQ), F32), pltpu.VMEM((1, BQ), F32),
                        pltpu.VMEM((V_HEAD, BQ), F32)],
        compiler_params=_params(2),
        name="attn",
    )(q_all, k_all, vt_all, k_all, vt_all)


def _wo_kernel(ot_ref, wo_ref, x_ref, o_ref):
    y = lax.dot_general(ot_ref[...], wo_ref[...], (((0,), (0,)), ((), ())),
                        preferred_element_type=F32)
    o_ref[...] = x_ref[...] + y


def _wo(ot, wo, h):
    row_spec = pl.BlockSpec((TM, D), lambda i: (i, 0))
    return pl.pallas_call(
        _wo_kernel,
        grid=(NT,),
        in_specs=[pl.BlockSpec((H * V_HEAD, TM), lambda i: (0, i)),
                  _const_spec((H * V_HEAD, D)), row_spec],
        out_specs=row_spec,
        out_shape=jax.ShapeDtypeStruct((ROWS, D), F32),
        compiler_params=_params(),
        name="attn_out",
    )(ot, wo, h)


def _rot_cols(w):
    half = QK_ROPE // 2
    return jnp.concatenate([-w[:, half:], w[:, :half]], axis=1)


def _place_rope(w):
    z_lo = jnp.zeros((w.shape[0], QK_NOPE), w.dtype)
    z_hi = jnp.zeros((w.shape[0], LANES - QK_NOPE - QK_ROPE), w.dtype)
    return jnp.concatenate([z_lo, w, z_hi], axis=1)


def _rope_tables():
    inv = 1.0 / (ROPE_THETA ** (jnp.arange(0, QK_ROPE, 2, dtype=F32) / QK_ROPE))
    row = jnp.arange(ROWS)
    pos = jnp.where(row < META_ROW0, N_META + row % SEQ, row - META_ROW0).astype(F32)
    ang = pos[:, None] * inv[None, :]
    ones = jnp.ones((ROWS, QK_NOPE), F32)
    ones_hi = jnp.ones((ROWS, LANES - QK_NOPE - QK_ROPE), F32)
    cos_t = jnp.concatenate([ones, jnp.cos(ang), jnp.cos(ang), ones_hi], axis=1)
    sin_t = jnp.concatenate([0 * ones, jnp.sin(ang), jnp.sin(ang), 0 * ones_hi], axis=1)
    return cos_t, sin_t


def kernel(x, meta_tokens, ffn1_norm, ffn1_w_gate, ffn1_w_up, ffn1_w_down, mix_norm, ffn2_norm, ffn2_w_gate, ffn2_w_up, ffn2_w_down, pool_w, pool_scale, kv_in_norm, w_dkv, kv_latent_norm, w_uk, w_uv, w_dq, q_latent_norm, w_uq, w_o, final_norm):
    cos_t, sin_t = _rope_tables()
    h = jnp.concatenate(
        [x.reshape(B * SEQ, D), meta_tokens.astype(x.dtype),
         jnp.zeros((TM - N_META, D), x.dtype)], axis=0)

    w_kr = w_dkv[:, KV_RANK:]
    wd_kv = jnp.concatenate(
        [w_dkv[:, :KV_RANK], _place_rope(w_kr), _place_rope(_rot_cols(w_kr))], axis=1).astype(BF16)
    wuk = jnp.pad(w_uk.reshape(KV_RANK, H, QK_NOPE),
                  ((0, 0), (0, 0), (0, LANES - QK_NOPE))).reshape(KV_RANK, H * LANES).astype(BF16)
    wuvt = w_uv.T.astype(BF16)

    k_all = vt_all = None
    out = None
    for l in range(DEPTH):
        h = _ffn(h, ffn1_norm[l], ffn1_w_gate[l].astype(BF16), ffn1_w_up[l].astype(BF16),
                 ffn1_w_down[l].astype(BF16))
        if l < N_A:
            h = _pool(h, mix_norm[l], pool_w[l].astype(BF16), pool_scale[l])
        else:
            jj = l - N_A
            wq3 = w_uq[jj].reshape(Q_RANK, H, QK_NOPE + QK_ROPE)
            wq_rope = wq3[:, :, QK_NOPE:]
            half = QK_ROPE // 2
            wq_rot = jnp.concatenate([-wq_rope[:, :, half:], wq_rope[:, :, :half]], axis=2)
            pad_hi = ((0, 0), (0, 0), (0, LANES - QK_NOPE - QK_ROPE))
            wq = jnp.pad(wq3, pad_hi).reshape(Q_RANK, H * LANES).astype(BF16)
            wqr = jnp.pad(wq_rot, ((0, 0), (0, 0), (QK_NOPE, LANES - QK_NOPE - QK_ROPE))
                          ).reshape(Q_RANK, H * LANES).astype(BF16)
            q_all = _q(h, mix_norm[l], w_dq[jj].astype(BF16), q_latent_norm[jj], wq, wqr,
                       cos_t, sin_t)
            ot = _attn(q_all, k_all, vt_all)
            h = _wo(ot, w_o[jj].astype(BF16), h)
        last = l == DEPTH - 1
        res = _ffn(h, ffn2_norm[l], ffn2_w_gate[l].astype(BF16), ffn2_w_up[l].astype(BF16),
                   ffn2_w_down[l].astype(BF16), final_g=final_norm if last else None)
        if last:
            out = res
        else:
            h = res
        if l == N_A - 1:
            k_all, vt_all = _kv(h, kv_in_norm, wd_kv, kv_latent_norm, wuk, wuvt, cos_t, sin_t)
    return out.reshape(B, SEQ, D)
```

```python
import functools
import math

import jax
import jax.numpy as jnp
from jax import lax
from jax.experimental import pallas as pl
from jax.experimental.pallas import tpu as pltpu

D = 1024
B = 2
SEQ = 8192
DEPTH = 4
CHUNK = 64
N_META = 16
N_A = DEPTH // 2
D_FF = 2816
POOL_WINDOWS = (2, 4, 8, 16)
POOL_GROUP = D // len(POOL_WINDOWS)
H = 8
QK_NOPE = 64
QK_ROPE = 32
V_HEAD = 64
KV_RANK = 256
Q_RANK = 384
ROPE_THETA = 10000.0
EPS = 1e-6

F32 = jnp.float32
BF16 = jnp.bfloat16

LANES = 128
TM = 512
NT_REAL = B * SEQ // TM
NT = NT_REAL + 1
ROWS = NT * TM
TILES_PER_SEQ = SEQ // TM
META_ROW0 = B * SEQ
FC = 256
KB = 256
BQ = TM
NS = BQ // KB
HP = 4
LOOKAHEAD = 5
HALO = 16
NEG = -1e30
VR = V_HEAD + 16
VMEM_LIMIT = 56 * 1024 * 1024


def _rms(x, g):
    ms = jnp.mean(x * x, axis=-1, keepdims=True)
    return (x * lax.rsqrt(ms + EPS)) * g


def _params(n_axes=1, flags=None):
    return pltpu.CompilerParams(
        dimension_semantics=("arbitrary",) * n_axes, vmem_limit_bytes=VMEM_LIMIT, flags=flags)


def _const_spec(shape, n_axes=1):
    zeros = (0,) * len(shape)
    if n_axes == 1:
        return pl.BlockSpec(shape, lambda i: zeros, pipeline_mode=pl.Buffered(1))
    return pl.BlockSpec(shape, lambda i, j: zeros, pipeline_mode=pl.Buffered(1))


def _ffn_kernel(x_ref, g_ref, wg_ref, wu_ref, wd_ref, *rest, final):
    if final:
        gf_ref, o_ref, xn_ref, a_ref = rest
    else:
        o_ref, xn_ref, a_ref = rest
    x = x_ref[...]
    xn_ref[...] = _rms(x, g_ref[...]).astype(BF16)
    for c in range(D_FF // FC):
        sl = slice(c * FC, (c + 1) * FC)
        gate = jnp.dot(xn_ref[...], wg_ref[:, sl], preferred_element_type=F32)
        up = jnp.dot(xn_ref[...], wu_ref[:, sl], preferred_element_type=F32)
        a_ref[:, sl] = (gate * jax.nn.sigmoid(gate) * up).astype(BF16)
    y = x + 0.5 * jnp.dot(a_ref[...], wd_ref[...], preferred_element_type=F32)
    if final:
        y = _rms(y, gf_ref[...])
    o_ref[...] = y


def _ffn(h, g, wg, wu, wd, final_g=None):
    final = final_g is not None
    n_tiles = NT_REAL if final else NT
    row_spec = pl.BlockSpec((TM, D), lambda i: (i, 0))
    in_specs = [row_spec, _const_spec((1, D)), _const_spec((D, D_FF)),
                _const_spec((D, D_FF)), _const_spec((D_FF, D))]
    args = [h, g.reshape(1, D), wg, wu, wd]
    if final:
        in_specs.append(_const_spec((1, D)))
        args.append(final_g.reshape(1, D))
    return pl.pallas_call(
        functools.partial(_ffn_kernel, final=final),
        grid=(n_tiles,),
        in_specs=in_specs,
        out_specs=row_spec,
        out_shape=jax.ShapeDtypeStruct((n_tiles * TM, D), F32),
        scratch_shapes=[pltpu.VMEM((TM, D), BF16), pltpu.VMEM((TM, D_FF), BF16)],
        compiler_params=_params(),
        name="ffn_final" if final else "ffn",
    )(*args)


def _pool_kernel(x_ref, halo_ref, g_ref, w_ref, s_ref, o_ref):
    i = pl.program_id(0)
    is_meta = i == NT - 1
    x = x_ref[...]
    g = g_ref[...]
    u = _rms(x, g)
    uh = _rms(halo_ref[...], g) * jnp.where(is_meta, 0.0, 1.0)
    pos = lax.broadcasted_iota(jnp.int32, (TM, POOL_GROUP), 0) + jnp.where(is_meta, 0, N_META)
    pad = jnp.zeros((HALO, POOL_GROUP), F32)
    for gi, w in enumerate(POOL_WINDOWS):
        sl = slice(gi * POOL_GROUP, (gi + 1) * POOL_GROUP)
        ug = u[:, sl]
        acc = jnp.concatenate([pad, uh[:, sl], ug], axis=0)
        step = 1
        while step < w:
            acc = acc + pltpu.roll(acc, step, 0)
            step *= 2
        cnt = jnp.minimum(pos + 1, w).astype(F32)
        pooled = acc[2 * HALO:, :] / cnt - ug
        y = jnp.dot(pooled.astype(BF16), w_ref[gi], preferred_element_type=F32)
        o_ref[:, sl] = x[:, sl] + y * s_ref[:, sl]


def _pool(h, g, w, scale):
    def halo_map(i):
        first = (i % TILES_PER_SEQ) == 0
        return (jnp.where(first, META_ROW0 // HALO, i * (TM // HALO) - 1), 0)

    row_spec = pl.BlockSpec((TM, D), lambda i: (i, 0))
    return pl.pallas_call(
        _pool_kernel,
        grid=(NT,),
        in_specs=[row_spec, pl.BlockSpec((HALO, D), halo_map), _const_spec((1, D)),
                  _const_spec((len(POOL_WINDOWS), POOL_GROUP, POOL_GROUP)), _const_spec((1, D))],
        out_specs=row_spec,
        out_shape=jax.ShapeDtypeStruct((ROWS, D), F32),
        compiler_params=_params(),
        name="pool",
    )(h, h, g.reshape(1, D), w, scale.reshape(1, D))


def _kv_kernel(x_ref, g_ref, wd_ref, gl_ref, wuk_ref, wuvt_ref, cos_ref, sin_ref, k_ref, vt_ref):
    xn = _rms(x_ref[...], g_ref[...]).astype(BF16)
    ckr = jnp.dot(xn, wd_ref[...], preferred_element_type=F32)
    ckv = _rms(ckr[:, :KV_RANK], gl_ref[...]).astype(BF16)
    kr = (ckr[:, KV_RANK:KV_RANK + LANES] * cos_ref[...]
          + ckr[:, KV_RANK + LANES:] * sin_ref[...])
    kn = jnp.dot(ckv, wuk_ref[...], preferred_element_type=F32)
    for hh in range(H):
        k_ref[hh] = (kn[:, hh * LANES:(hh + 1) * LANES] + kr).astype(BF16)
    vt = lax.dot_general(wuvt_ref[...], ckv, (((1,), (1,)), ((), ())),
                         preferred_element_type=F32)
    ones_row = (lax.broadcasted_iota(jnp.int32, (VR - V_HEAD, TM), 0) == 0).astype(BF16)
    for hh in range(H):
        vt_ref[hh * VR:hh * VR + V_HEAD, :] = vt[hh * V_HEAD:(hh + 1) * V_HEAD, :].astype(BF16)
        vt_ref[hh * VR + V_HEAD:(hh + 1) * VR, :] = ones_row


def _kv(h, g, wd, gl, wuk, wuvt, cos_t, sin_t):
    row_spec = pl.BlockSpec((TM, D), lambda i: (i, 0))
    tab_spec = pl.BlockSpec((TM, LANES), lambda i: (i, 0))
    return pl.pallas_call(
        _kv_kernel,
        grid=(NT,),
        in_specs=[row_spec, _const_spec((1, D)), _const_spec((D, KV_RANK + 2 * LANES)),
                  _const_spec((1, KV_RANK)), _const_spec((KV_RANK, H * LANES)),
                  _const_spec((H * V_HEAD, KV_RANK)), tab_spec, tab_spec],
        out_specs=[pl.BlockSpec((H, TM, LANES), lambda i: (0, i, 0)),
                   pl.BlockSpec((H * VR, TM), lambda i: (0, i))],
        out_shape=[jax.ShapeDtypeStruct((H, ROWS, LANES), BF16),
                   jax.ShapeDtypeStruct((H * VR, ROWS), BF16)],
        compiler_params=_params(),
        name="shared_kv",
    )(h, g.reshape(1, D), wd, gl.reshape(1, KV_RANK), wuk, wuvt, cos_t, sin_t)


def _q_kernel(x_ref, g_ref, wdq_ref, gl_ref, wq_ref, wqr_ref, cos_ref, sin_ref, q_ref):
    u = _rms(x_ref[...], g_ref[...]).astype(BF16)
    cq = jnp.dot(u, wdq_ref[...], preferred_element_type=F32)
    cqn = _rms(cq, gl_ref[...]).astype(BF16)
    qa = jnp.dot(cqn, wq_ref[...], preferred_element_type=F32)
    qb = jnp.dot(cqn, wqr_ref[...], preferred_element_type=F32)
    cos = cos_ref[...]
    sin = sin_ref[...]
    scale = (1.0 / math.sqrt(QK_NOPE + QK_ROPE)) * math.log2(math.e)
    for hh in range(H):
        sl = slice(hh * LANES, (hh + 1) * LANES)
        q_ref[hh] = ((qa[:, sl] * cos + qb[:, sl] * sin) * scale).astype(BF16)


def _q(h, g, wdq, gl, wq, wqr, cos_t, sin_t):
    row_spec = pl.BlockSpec((TM, D), lambda i: (i, 0))
    tab_spec = pl.BlockSpec((TM, LANES), lambda i: (i, 0))
    return pl.pallas_call(
        _q_kernel,
        grid=(NT,),
        in_specs=[row_spec, _const_spec((1, D)), _const_spec((D, Q_RANK)),
                  _const_spec((1, Q_RANK)), _const_spec((Q_RANK, H * LANES)),
                  _const_spec((Q_RANK, H * LANES)), tab_spec, tab_spec],
        out_specs=pl.BlockSpec((H, TM, LANES), lambda i: (0, i, 0)),
        out_shape=jax.ShapeDtypeStruct((H, ROWS, LANES), BF16),
        compiler_params=_params(),
        name="q_proj",
    )(h, g.reshape(1, D), wdq, gl.reshape(1, Q_RANK), wq, wqr, cos_t, sin_t)


def _attn_kernel(q_ref, k_ref, vt_ref, km_ref, vmt_ref, o_ref, m_ref, acc_ref):
    j = pl.program_id(1)
    is_real = j < NT_REAL
    nt_dims = (((1,), (1,)), ((), ()))

    def head_rows(hh):
        return slice(hh * VR, (hh + 1) * VR)

    for hh in range(HP):
        s = lax.dot_general(km_ref[hh], q_ref[hh], nt_dims, preferred_element_type=F32)
        kidx = lax.broadcasted_iota(jnp.int32, s.shape, 0)
        s = jnp.where(kidx < N_META, s, NEG)
        m0 = jnp.max(s, axis=0, keepdims=True)
        p = jnp.exp2((s - m0).astype(BF16))
        m_ref[hh] = m0
        acc_ref[head_rows(hh), :] = jnp.dot(vmt_ref[head_rows(hh), :], p,
                                            preferred_element_type=F32)

    def run_tiles(tiles):
        chains = sorted({(hh, c) for hh, c, _, _, _ in tiles})
        state = {}
        for hh, c in chains:
            cs = slice(c * KB, (c + 1) * KB)
            state[hh, c] = (m_ref[hh, :, cs], acc_ref[head_rows(hh), cs])
        scores = {}

        def issue(t):
            hh, c, kblk, _, mask = tiles[t]
            s = lax.dot_general(kblk, q_ref[hh, c * KB:(c + 1) * KB, :], nt_dims,
                                preferred_element_type=F32)
            scores[t] = s if mask is None else jnp.where(mask, s, NEG)

        for t in range(min(LOOKAHEAD, len(tiles))):
            issue(t)
        for t, (hh, c, _, vtblk, _) in enumerate(tiles):
            m_old, acc = state[hh, c]
            s = scores.pop(t)
            m_new = jnp.maximum(m_old, jnp.max(s, axis=0, keepdims=True))
            alpha = jnp.exp2(m_old - m_new)
            p = jnp.exp2((s - m_new).astype(BF16))
            if t + LOOKAHEAD < len(tiles):
                issue(t + LOOKAHEAD)
            acc = alpha * acc + jnp.dot(vtblk, p, preferred_element_type=F32)
            state[hh, c] = (m_new, acc)
        for hh, c in chains:
            cs = slice(c * KB, (c + 1) * KB)
            m_ref[hh, :, cs], acc_ref[head_rows(hh), cs] = state[hh, c]

    def load(hh, kb):
        off = pl.multiple_of(kb * KB, KB)
        return k_ref[hh, pl.ds(off, KB), :], vt_ref[head_rows(hh), pl.ds(off, KB)]

    q_tile = j % TILES_PER_SEQ
    first_diag = q_tile * NS
    n_iter = jnp.where(is_real, q_tile, 0)

    def full_body(it, carry):
        tiles = []
        for u in range(NS):
            for hh in range(HP):
                kblk, vtblk = load(hh, it * NS + u)
                tiles += [(hh, c, kblk, vtblk, None) for c in range(NS)]
        run_tiles(tiles)
        return carry

    lax.fori_loop(0, n_iter, full_body, 0)

    @pl.when(is_real)
    def _():
        kpos = lax.broadcasted_iota(jnp.int32, (KB, KB), 0)
        qpos = lax.broadcasted_iota(jnp.int32, (KB, KB), 1)
        chunk_mask = (kpos // CHUNK) <= (qpos // CHUNK)
        tiles = []
        for d in range(NS):
            for hh in range(HP):
                kblk, vtblk = load(hh, first_diag + d)
                tiles += [(hh, c, kblk, vtblk, chunk_mask if c == d else None)
                          for c in range(d, NS)]
        run_tiles(tiles)

    for hh in range(HP):
        acc = acc_ref[head_rows(hh), :]
        o_ref[hh * V_HEAD:(hh + 1) * V_HEAD, :] = (
            acc[:V_HEAD, :] / acc[V_HEAD:V_HEAD + 1, :]).astype(BF16)


def _attn(q_all, k_all, vt_all):
    def b_of(j):
        return jnp.minimum(j // TILES_PER_SEQ, B - 1)

    return pl.pallas_call(
        _attn_kernel,
        grid=(H // HP, NT),
        in_specs=[pl.BlockSpec((HP, BQ, LANES), lambda h, j: (h, j, 0)),
                  pl.BlockSpec((HP, SEQ, LANES), lambda h, j: (h, b_of(j), 0)),
                  pl.BlockSpec((HP * VR, SEQ), lambda h, j: (h, b_of(j))),
                  pl.BlockSpec((HP, LANES, LANES), lambda h, j: (h, META_ROW0 // LANES, 0)),
                  pl.BlockSpec((HP * VR, LANES), lambda h, j: (h, META_ROW0 // LANES))],
        out_specs=pl.BlockSpec((HP * V_HEAD, BQ), lambda h, j: (h, j)),
        out_shape=jax.ShapeDtypeStruct((H * V_HEAD, ROWS), BF16),
        scratch_shapes=[pltpu.VMEM((HP, 1, BQ), F32), pltpu.VMEM((HP * VR, BQ), F32)],
        compiler_params=_params(2),
        name="attn",
    )(q_all, k_all, vt_all, k_all, vt_all)


def _wo_kernel(ot_ref, wo_ref, x_ref, o_ref):
    y = lax.dot_general(ot_ref[...], wo_ref[...], (((0,), (0,)), ((), ())),
                        preferred_element_type=F32)
    o_ref[...] = x_ref[...] + y


def _wo(ot, wo, h):
    row_spec = pl.BlockSpec((TM, D), lambda i: (i, 0))
    return pl.pallas_call(
        _wo_kernel,
        grid=(NT,),
        in_specs=[pl.BlockSpec((H * V_HEAD, TM), lambda i: (0, i)),
                  _const_spec((H * V_HEAD, D)), row_spec],
        out_specs=row_spec,
        out_shape=jax.ShapeDtypeStruct((ROWS, D), F32),
        compiler_params=_params(),
        name="attn_out",
    )(ot, wo, h)


def _rot_cols(w):
    half = QK_ROPE // 2
    return jnp.concatenate([-w[:, half:], w[:, :half]], axis=1)


def _place_rope(w):
    z_lo = jnp.zeros((w.shape[0], QK_NOPE), w.dtype)
    z_hi = jnp.zeros((w.shape[0], LANES - QK_NOPE - QK_ROPE), w.dtype)
    return jnp.concatenate([z_lo, w, z_hi], axis=1)


def _rope_tables():
    inv = 1.0 / (ROPE_THETA ** (jnp.arange(0, QK_ROPE, 2, dtype=F32) / QK_ROPE))
    row = jnp.arange(ROWS)
    pos = jnp.where(row < META_ROW0, N_META + row % SEQ, row - META_ROW0).astype(F32)
    ang = pos[:, None] * inv[None, :]
    ones = jnp.ones((ROWS, QK_NOPE), F32)
    ones_hi = jnp.ones((ROWS, LANES - QK_NOPE - QK_ROPE), F32)
    cos_t = jnp.concatenate([ones, jnp.cos(ang), jnp.cos(ang), ones_hi], axis=1)
    sin_t = jnp.concatenate([0 * ones, jnp.sin(ang), jnp.sin(ang), 0 * ones_hi], axis=1)
    return cos_t, sin_t


def kernel(x, meta_tokens, ffn1_norm, ffn1_w_gate, ffn1_w_up, ffn1_w_down, mix_norm, ffn2_norm, ffn2_w_gate, ffn2_w_up, ffn2_w_down, pool_w, pool_scale, kv_in_norm, w_dkv, kv_latent_norm, w_uk, w_uv, w_dq, q_latent_norm, w_uq, w_o, final_norm):
    cos_t, sin_t = _rope_tables()
    h = jnp.concatenate(
        [x.reshape(B * SEQ, D), meta_tokens.astype(x.dtype),
         jnp.zeros((TM - N_META, D), x.dtype)], axis=0)

    w_kr = w_dkv[:, KV_RANK:]
    wd_kv = jnp.concatenate(
        [w_dkv[:, :KV_RANK], _place_rope(w_kr), _place_rope(_rot_cols(w_kr))], axis=1).astype(BF16)
    wuk = jnp.pad(w_uk.reshape(KV_RANK, H, QK_NOPE),
                  ((0, 0), (0, 0), (0, LANES - QK_NOPE))).reshape(KV_RANK, H * LANES).astype(BF16)
    wuvt = w_uv.T.astype(BF16)

    k_all = vt_all = None
    out = None
    for l in range(DEPTH):
        h = _ffn(h, ffn1_norm[l], ffn1_w_gate[l].astype(BF16), ffn1_w_up[l].astype(BF16),
                 ffn1_w_down[l].astype(BF16))
        if l < N_A:
            h = _pool(h, mix_norm[l], pool_w[l].astype(BF16), pool_scale[l])
        else:
            jj = l - N_A
            wq3 = w_uq[jj].reshape(Q_RANK, H, QK_NOPE + QK_ROPE)
            wq_rope = wq3[:, :, QK_NOPE:]
            half = QK_ROPE // 2
            wq_rot = jnp.concatenate([-wq_rope[:, :, half:], wq_rope[:, :, :half]], axis=2)
            pad_hi = ((0, 0), (0, 0), (0, LANES - QK_NOPE - QK_ROPE))
            wq = jnp.pad(wq3, pad_hi).reshape(Q_RANK, H * LANES).astype(BF16)
            wqr = jnp.pad(wq_rot, ((0, 0), (0, 0), (QK_NOPE, LANES - QK_NOPE - QK_ROPE))
                          ).reshape(Q_RANK, H * LANES).astype(BF16)
            q_all = _q(h, mix_norm[l], w_dq[jj].astype(BF16), q_latent_norm[jj], wq, wqr,
                       cos_t, sin_t)
            ot = _attn(q_all, k_all, vt_all)
            h = _wo(ot, w_o[jj].astype(BF16), h)
        last = l == DEPTH - 1
        res = _ffn(h, ffn2_norm[l], ffn2_w_gate[l].astype(BF16), ffn2_w_up[l].astype(BF16),
                   ffn2_w_down[l].astype(BF16), final_g=final_norm if last else None)
        if last:
            out = res
        else:
            h = res
        if l == N_A - 1:
            k_all, vt_all = _kv(h, kv_in_norm, wd_kv, kv_latent_norm, wuk, wuvt, cos_t, sin_t)
    return out.reshape(B, SEQ, D)
```

```python
import functools
import math

import jax
import jax.numpy as jnp
from jax import lax
from jax.experimental import pallas as pl
from jax.experimental.pallas import tpu as pltpu

D = 1024
B = 2
SEQ = 8192
DEPTH = 4
CHUNK = 64
N_META = 16
N_A = DEPTH // 2
D_FF = 2816
POOL_WINDOWS = (2, 4, 8, 16)
POOL_GROUP = D // len(POOL_WINDOWS)
H = 8
QK_NOPE = 64
QK_ROPE = 32
V_HEAD = 64
KV_RANK = 256
Q_RANK = 384
ROPE_THETA = 10000.0
EPS = 1e-6

F32 = jnp.float32
BF16 = jnp.bfloat16

LANES = 128
TM = 512
NT_REAL = B * SEQ // TM
NT = NT_REAL + 1
ROWS = NT * TM
TILES_PER_SEQ = SEQ // TM
META_ROW0 = B * SEQ
TAB_ROWS = SEQ + TM
FC = 256
KB = 256
BQ = TM
NS = BQ // KB
HP = H
LOOKAHEAD = 5
HALO = 16
NEG = -1e30
VR = V_HEAD + 16
VMEM_LIMIT = 56 * 1024 * 1024


def _rms(x, g):
    ms = jnp.mean(x * x, axis=-1, keepdims=True)
    return (x * lax.rsqrt(ms + EPS)) * g


def _params(n_axes=1, flags=None):
    return pltpu.CompilerParams(
        dimension_semantics=("arbitrary",) * n_axes, vmem_limit_bytes=VMEM_LIMIT, flags=flags)


def _const_spec(shape, n_axes=1):
    zeros = (0,) * len(shape)
    if n_axes == 1:
        return pl.BlockSpec(shape, lambda i: zeros, pipeline_mode=pl.Buffered(1))
    return pl.BlockSpec(shape, lambda i, j: zeros, pipeline_mode=pl.Buffered(1))


def _ffn_kernel(x_ref, g_ref, wg_ref, wu_ref, wd_ref, *rest, first, final):
    rest = list(rest)
    xm_ref = rest.pop(0) if first else None
    gf_ref = rest.pop(0) if final else None
    o_ref, xn_ref, a_ref = rest
    x = x_ref[...]
    if first:
        x = jnp.where(pl.program_id(0) == NT - 1, xm_ref[...], x)
    xn_ref[...] = _rms(x, g_ref[...]).astype(BF16)
    for c in range(D_FF // FC):
        sl = slice(c * FC, (c + 1) * FC)
        gate = jnp.dot(xn_ref[...], wg_ref[:, sl], preferred_element_type=F32)
        up = jnp.dot(xn_ref[...], wu_ref[:, sl], preferred_element_type=F32)
        a_ref[:, sl] = (gate * jax.nn.sigmoid(gate) * up).astype(BF16)
    y = x + 0.5 * jnp.dot(a_ref[...], wd_ref[...], preferred_element_type=F32)
    if final:
        y = _rms(y, gf_ref[...])
    o_ref[...] = y


def _layer_spec(shape, layer):
    return pl.BlockSpec((None,) + shape, lambda i: (layer, 0, 0), pipeline_mode=pl.Buffered(1))


def _ffn(h, g, wg, wu, wd, layer, meta_tile=None, final_g=None):
    first = meta_tile is not None
    final = final_g is not None
    n_tiles = NT_REAL if final else NT
    row_spec = pl.BlockSpec((TM, D), lambda i: (i, 0))
    x_spec = pl.BlockSpec((TM, D), lambda i: (jnp.minimum(i, NT_REAL - 1), 0)) if first else row_spec
    in_specs = [x_spec, _const_spec((1, D)), _layer_spec((D, D_FF), layer),
                _layer_spec((D, D_FF), layer), _layer_spec((D_FF, D), layer)]
    args = [h, g.reshape(1, D), wg, wu, wd]
    if first:
        in_specs.append(_const_spec((TM, D)))
        args.append(meta_tile)
    if final:
        in_specs.append(_const_spec((1, D)))
        args.append(final_g.reshape(1, D))
    return pl.pallas_call(
        functools.partial(_ffn_kernel, first=first, final=final),
        grid=(n_tiles,),
        in_specs=in_specs,
        out_specs=row_spec,
        out_shape=jax.ShapeDtypeStruct((n_tiles * TM, D), F32),
        scratch_shapes=[pltpu.VMEM((TM, D), BF16), pltpu.VMEM((TM, D_FF), BF16)],
        compiler_params=_params(),
        name="ffn_final" if final else "ffn",
    )(*args)


def _pool_kernel(x_ref, halo_ref, g_ref, w_ref, s_ref, o_ref):
    i = pl.program_id(0)
    is_meta = i == NT - 1
    x = x_ref[...]
    g = g_ref[...]
    u = _rms(x, g)
    uh = _rms(halo_ref[...], g) * jnp.where(is_meta, 0.0, 1.0)
    pos = lax.broadcasted_iota(jnp.int32, (TM, POOL_GROUP), 0) + jnp.where(is_meta, 0, N_META)
    pad = jnp.zeros((HALO, POOL_GROUP), F32)
    for gi, w in enumerate(POOL_WINDOWS):
        sl = slice(gi * POOL_GROUP, (gi + 1) * POOL_GROUP)
        ug = u[:, sl]
        acc = jnp.concatenate([pad, uh[:, sl], ug], axis=0)
        step = 1
        while step < w:
            acc = acc + pltpu.roll(acc, step, 0)
            step *= 2
        cnt = jnp.minimum(pos + 1, w).astype(F32)
        pooled = acc[2 * HALO:, :] / cnt - ug
        y = jnp.dot(pooled.astype(BF16), w_ref[gi], preferred_element_type=F32)
        o_ref[:, sl] = x[:, sl] + y * s_ref[:, sl]


def _pool(h, g, w, scale):
    def halo_map(i):
        first = (i % TILES_PER_SEQ) == 0
        return (jnp.where(first, META_ROW0 // HALO, i * (TM // HALO) - 1), 0)

    row_spec = pl.BlockSpec((TM, D), lambda i: (i, 0))
    return pl.pallas_call(
        _pool_kernel,
        grid=(NT,),
        in_specs=[row_spec, pl.BlockSpec((HALO, D), halo_map), _const_spec((1, D)),
                  _const_spec((len(POOL_WINDOWS), POOL_GROUP, POOL_GROUP)), _const_spec((1, D))],
        out_specs=row_spec,
        out_shape=jax.ShapeDtypeStruct((ROWS, D), F32),
        compiler_params=_params(),
        name="pool",
    )(h, h, g.reshape(1, D), w, scale.reshape(1, D))


def _tab_map(i):
    return (jnp.where(i < NT_REAL, i % TILES_PER_SEQ, TILES_PER_SEQ), 0)


def _kv_kernel(x_ref, g_ref, wd_ref, gl_ref, wuk_ref, wuvt_ref, cos_ref, sin_ref, k_ref, vt_ref):
    xn = _rms(x_ref[...], g_ref[...]).astype(BF16)
    ckr = jnp.dot(xn, wd_ref[...], preferred_element_type=F32)
    ckv = _rms(ckr[:, :KV_RANK], gl_ref[...]).astype(BF16)
    kr = (ckr[:, KV_RANK:KV_RANK + LANES] * cos_ref[...]
          + ckr[:, KV_RANK + LANES:] * sin_ref[...])
    kn = jnp.dot(ckv, wuk_ref[...], preferred_element_type=F32)
    for hh in range(H):
        k_ref[hh] = (kn[:, hh * LANES:(hh + 1) * LANES] + kr).astype(BF16)
    vt = lax.dot_general(wuvt_ref[...], ckv, (((1,), (1,)), ((), ())),
                         preferred_element_type=F32)
    ones_row = (lax.broadcasted_iota(jnp.int32, (VR - V_HEAD, TM), 0) == 0).astype(BF16)
    for hh in range(H):
        vt_ref[hh * VR:hh * VR + V_HEAD, :] = vt[hh * V_HEAD:(hh + 1) * V_HEAD, :].astype(BF16)
        vt_ref[hh * VR + V_HEAD:(hh + 1) * VR, :] = ones_row


def _kv(h, g, wd, gl, wuk, wuvt, cos_t, sin_t):
    row_spec = pl.BlockSpec((TM, D), lambda i: (i, 0))
    tab_spec = pl.BlockSpec((TM, LANES), _tab_map)
    return pl.pallas_call(
        _kv_kernel,
        grid=(NT,),
        in_specs=[row_spec, _const_spec((1, D)), _const_spec((D, KV_RANK + 2 * LANES)),
                  _const_spec((1, KV_RANK)), _const_spec((KV_RANK, H * LANES)),
                  _const_spec((H * V_HEAD, KV_RANK)), tab_spec, tab_spec],
        out_specs=[pl.BlockSpec((H, TM, LANES), lambda i: (0, i, 0)),
                   pl.BlockSpec((H * VR, TM), lambda i: (0, i))],
        out_shape=[jax.ShapeDtypeStruct((H, ROWS, LANES), BF16),
                   jax.ShapeDtypeStruct((H * VR, ROWS), BF16)],
        compiler_params=_params(),
        name="shared_kv",
    )(h, g.reshape(1, D), wd, gl.reshape(1, KV_RANK), wuk, wuvt, cos_t, sin_t)


def _q_kernel(x_ref, g_ref, wdq_ref, gl_ref, wq_ref, wqr_ref, cos_ref, sin_ref, q_ref):
    u = _rms(x_ref[...], g_ref[...]).astype(BF16)
    cq = jnp.dot(u, wdq_ref[...], preferred_element_type=F32)
    cqn = _rms(cq, gl_ref[...]).astype(BF16)
    qa = jnp.dot(cqn, wq_ref[...], preferred_element_type=F32)
    qb = jnp.dot(cqn, wqr_ref[...], preferred_element_type=F32)
    cos = cos_ref[...]
    sin = sin_ref[...]
    scale = (1.0 / math.sqrt(QK_NOPE + QK_ROPE)) * math.log2(math.e)
    for hh in range(H):
        sl = slice(hh * LANES, (hh + 1) * LANES)
        q_ref[hh] = ((qa[:, sl] * cos + qb[:, sl] * sin) * scale).astype(BF16)


def _q(h, g, wdq, gl, wq, wqr, cos_t, sin_t):
    row_spec = pl.BlockSpec((TM, D), lambda i: (i, 0))
    tab_spec = pl.BlockSpec((TM, LANES), _tab_map)
    return pl.pallas_call(
        _q_kernel,
        grid=(NT,),
        in_specs=[row_spec, _const_spec((1, D)), _const_spec((D, Q_RANK)),
                  _const_spec((1, Q_RANK)), _const_spec((Q_RANK, H * LANES)),
                  _const_spec((Q_RANK, H * LANES)), tab_spec, tab_spec],
        out_specs=pl.BlockSpec((H, TM, LANES), lambda i: (0, i, 0)),
        out_shape=jax.ShapeDtypeStruct((H, ROWS, LANES), BF16),
        compiler_params=_params(),
        name="q_proj",
    )(h, g.reshape(1, D), wdq, gl.reshape(1, Q_RANK), wq, wqr, cos_t, sin_t)


def _attn_kernel(q_ref, k_ref, vt_ref, km_ref, vmt_ref, o_ref, m_ref, acc_ref, s_ref):
    j = pl.program_id(1)
    is_real = j < NT_REAL
    nt_dims = (((1,), (1,)), ((), ()))

    def head_rows(hh):
        return slice(hh * VR, (hh + 1) * VR)

    for hh in range(HP):
        s = lax.dot_general(km_ref[hh], q_ref[hh], nt_dims, preferred_element_type=F32)
        kidx = lax.broadcasted_iota(jnp.int32, s.shape, 0)
        s = jnp.where(kidx < N_META, s, NEG)
        m0 = jnp.max(s, axis=0, keepdims=True)
        p = jnp.exp2((s - m0).astype(BF16))
        m_ref[hh] = m0
        acc_ref[head_rows(hh), :] = jnp.dot(vmt_ref[head_rows(hh), :], p,
                                            preferred_element_type=F32)

    def qk(hh, c, kb):
        off = pl.multiple_of(kb * KB, KB)
        return lax.dot_general(k_ref[hh, pl.ds(off, KB), :], q_ref[hh, c * KB:(c + 1) * KB, :],
                               nt_dims, preferred_element_type=F32)

    def run_tiles(tiles, kb0, tail):
        chains = sorted({(hh, c) for hh, c, _, _ in tiles})
        state = {}
        for hh, c in chains:
            cs = slice(c * KB, (c + 1) * KB)
            state[hh, c] = (m_ref[hh, :, cs], acc_ref[head_rows(hh), cs])
        scores = {}
        pending = [(hh, c, kb0 + u) for hh, c, u, _ in tiles[LOOKAHEAD:]] + list(tail)
        n_issued = LOOKAHEAD
        for t, (hh, c, u, masked) in enumerate(tiles):
            m_old, acc = state[hh, c]
            s = s_ref[t] if t < LOOKAHEAD else scores.pop(t)
            if masked:
                s = jnp.where(chunk_mask, s, NEG)
            m_new = jnp.maximum(m_old, jnp.max(s, axis=0, keepdims=True))
            alpha = jnp.exp2(m_old - m_new)
            p = jnp.exp2((s - m_new).astype(BF16))
            if pending:
                nxt = qk(*pending.pop(0))
                if n_issued < len(tiles):
                    scores[n_issued] = nxt
                else:
                    s_ref[n_issued - len(tiles)] = nxt
                n_issued += 1
            off = pl.multiple_of((kb0 + u) * KB, KB)
            acc = alpha * acc + jnp.dot(vt_ref[head_rows(hh), pl.ds(off, KB)], p,
                                        preferred_element_type=F32)
            state[hh, c] = (m_new, acc)
        for hh, c in chains:
            cs = slice(c * KB, (c + 1) * KB)
            m_ref[hh, :, cs], acc_ref[head_rows(hh), cs] = state[hh, c]

    kpos = lax.broadcasted_iota(jnp.int32, (KB, KB), 0)
    qpos = lax.broadcasted_iota(jnp.int32, (KB, KB), 1)
    chunk_mask = (kpos // CHUNK) <= (qpos // CHUNK)

    full_tiles = [(hh, c, u, False) for u in range(NS) for hh in range(HP) for c in range(NS)]
    diag_tiles = [(hh, c, d, c == d) for d in range(NS) for hh in range(HP) for c in range(d, NS)]
    assert [t[:3] for t in full_tiles[:LOOKAHEAD]] == [t[:3] for t in diag_tiles[:LOOKAHEAD]]

    def first_scores(kb0):
        return [(hh, c, kb0 + u) for hh, c, u, _ in full_tiles[:LOOKAHEAD]]

    q_tile = j % TILES_PER_SEQ
    n_iter = jnp.where(is_real, q_tile, 0)

    def full_body(it, carry):
        run_tiles(full_tiles, it * NS, first_scores((it + 1) * NS))
        return carry

    for t, a in enumerate(first_scores(0)):
        s_ref[t] = qk(*a)
    lax.fori_loop(0, n_iter, full_body, 0)

    @pl.when(is_real)
    def _():
        run_tiles(diag_tiles, q_tile * NS, [])

    for hh in range(HP):
        acc = acc_ref[head_rows(hh), :]
        o_ref[hh * V_HEAD:(hh + 1) * V_HEAD, :] = (
            acc[:V_HEAD, :] / acc[V_HEAD:V_HEAD + 1, :]).astype(BF16)


def _attn(q_all, k_all, vt_all):
    def b_of(j):
        return jnp.minimum(j // TILES_PER_SEQ, B - 1)

    return pl.pallas_call(
        _attn_kernel,
        grid=(H // HP, NT),
        in_specs=[pl.BlockSpec((HP, BQ, LANES), lambda h, j: (h, j, 0)),
                  pl.BlockSpec((HP, SEQ, LANES), lambda h, j: (h, b_of(j), 0),
                               pipeline_mode=pl.Buffered(1)),
                  pl.BlockSpec((HP * VR, SEQ), lambda h, j: (h, b_of(j)),
                               pipeline_mode=pl.Buffered(1)),
                  pl.BlockSpec((HP, LANES, LANES), lambda h, j: (h, META_ROW0 // LANES, 0)),
                  pl.BlockSpec((HP * VR, LANES), lambda h, j: (h, META_ROW0 // LANES))],
        out_specs=pl.BlockSpec((HP * V_HEAD, BQ), lambda h, j: (h, j)),
        out_shape=jax.ShapeDtypeStruct((H * V_HEAD, ROWS), BF16),
        scratch_shapes=[pltpu.VMEM((HP, 1, BQ), F32), pltpu.VMEM((HP * VR, BQ), F32),
                        pltpu.VMEM((LOOKAHEAD, KB, KB), F32)],
        compiler_params=_params(2),
        name="attn",
    )(q_all, k_all, vt_all, k_all, vt_all)


def _wo_kernel(ot_ref, wo_ref, x_ref, o_ref):
    y = lax.dot_general(ot_ref[...], wo_ref[...], (((0,), (0,)), ((), ())),
                        preferred_element_type=F32)
    o_ref[...] = x_ref[...] + y


def _wo(ot, wo, h):
    row_spec = pl.BlockSpec((TM, D), lambda i: (i, 0))
    return pl.pallas_call(
        _wo_kernel,
        grid=(NT,),
        in_specs=[pl.BlockSpec((H * V_HEAD, TM), lambda i: (0, i)),
                  _const_spec((H * V_HEAD, D)), row_spec],
        out_specs=row_spec,
        out_shape=jax.ShapeDtypeStruct((ROWS, D), F32),
        compiler_params=_params(),
        name="attn_out",
    )(ot, wo, h)


def _rot_cols(w):
    half = QK_ROPE // 2
    return jnp.concatenate([-w[:, half:], w[:, :half]], axis=1)


def _place_rope(w):
    z_lo = jnp.zeros((w.shape[0], QK_NOPE), w.dtype)
    z_hi = jnp.zeros((w.shape[0], LANES - QK_NOPE - QK_ROPE), w.dtype)
    return jnp.concatenate([z_lo, w, z_hi], axis=1)


def _rope_tables():
    inv = 1.0 / (ROPE_THETA ** (jnp.arange(0, QK_ROPE, 2, dtype=F32) / QK_ROPE))
    row = jnp.arange(TAB_ROWS)
    pos = jnp.where(row < SEQ, N_META + row, row - SEQ).astype(F32)
    ang = pos[:, None] * inv[None, :]
    ones = jnp.ones((TAB_ROWS, QK_NOPE), F32)
    ones_hi = jnp.ones((TAB_ROWS, LANES - QK_NOPE - QK_ROPE), F32)
    cos_t = jnp.concatenate([ones, jnp.cos(ang), jnp.cos(ang), ones_hi], axis=1)
    sin_t = jnp.concatenate([0 * ones, jnp.sin(ang), jnp.sin(ang), 0 * ones_hi], axis=1)
    return cos_t, sin_t


def kernel(x, meta_tokens, ffn1_norm, ffn1_w_gate, ffn1_w_up, ffn1_w_down, mix_norm, ffn2_norm, ffn2_w_gate, ffn2_w_up, ffn2_w_down, pool_w, pool_scale, kv_in_norm, w_dkv, kv_latent_norm, w_uk, w_uv, w_dq, q_latent_norm, w_uq, w_o, final_norm):
    cos_t, sin_t = _rope_tables()
    h = x.reshape(B * SEQ, D)
    meta_tile = jnp.concatenate(
        [meta_tokens.astype(x.dtype), jnp.zeros((TM - N_META, D), x.dtype)], axis=0)
    ffn1_w = [w.astype(BF16) for w in (ffn1_w_gate, ffn1_w_up, ffn1_w_down)]
    ffn2_w = [w.astype(BF16) for w in (ffn2_w_gate, ffn2_w_up, ffn2_w_down)]

    w_kr = w_dkv[:, KV_RANK:]
    wd_kv = jnp.concatenate(
        [w_dkv[:, :KV_RANK], _place_rope(w_kr), _place_rope(_rot_cols(w_kr))], axis=1).astype(BF16)
    wuk = jnp.pad(w_uk.reshape(KV_RANK, H, QK_NOPE),
                  ((0, 0), (0, 0), (0, LANES - QK_NOPE))).reshape(KV_RANK, H * LANES).astype(BF16)
    wuvt = w_uv.T.astype(BF16)

    k_all = vt_all = None
    out = None
    for l in range(DEPTH):
        h = _ffn(h, ffn1_norm[l], *ffn1_w, l, meta_tile=meta_tile if l == 0 else None)
        if l < N_A:
            h = _pool(h, mix_norm[l], pool_w[l].astype(BF16), pool_scale[l])
        else:
            jj = l - N_A
            wq3 = w_uq[jj].reshape(Q_RANK, H, QK_NOPE + QK_ROPE)
            wq_rope = wq3[:, :, QK_NOPE:]
            half = QK_ROPE // 2
            wq_rot = jnp.concatenate([-wq_rope[:, :, half:], wq_rope[:, :, :half]], axis=2)
            pad_hi = ((0, 0), (0, 0), (0, LANES - QK_NOPE - QK_ROPE))
            wq = jnp.pad(wq3, pad_hi).reshape(Q_RANK, H * LANES).astype(BF16)
            wqr = jnp.pad(wq_rot, ((0, 0), (0, 0), (QK_NOPE, LANES - QK_NOPE - QK_ROPE))
                          ).reshape(Q_RANK, H * LANES).astype(BF16)
            q_all = _q(h, mix_norm[l], w_dq[jj].astype(BF16), q_latent_norm[jj], wq, wqr,
                       cos_t, sin_t)
            ot = _attn(q_all, k_all, vt_all)
            h = _wo(ot, w_o[jj].astype(BF16), h)
        last = l == DEPTH - 1
        res = _ffn(h, ffn2_norm[l], *ffn2_w, l, final_g=final_norm if last else None)
        if last:
            out = res
        else:
            h = res
        if l == N_A - 1:
            k_all, vt_all = _kv(h, kv_in_norm, wd_kv, kv_latent_norm, wuk, wuvt, cos_t, sin_t)
    return out.reshape(B, SEQ, D)
```

```python
import functools
import math

import jax
import jax.numpy as jnp
from jax import lax
from jax.experimental import pallas as pl
from jax.experimental.pallas import tpu as pltpu

D = 1024
B = 2
SEQ = 8192
DEPTH = 4
CHUNK = 64
N_META = 16
N_A = DEPTH // 2
D_FF = 2816
POOL_WINDOWS = (2, 4, 8, 16)
POOL_GROUP = D // len(POOL_WINDOWS)
H = 8
QK_NOPE = 64
QK_ROPE = 32
V_HEAD = 64
KV_RANK = 256
Q_RANK = 384
ROPE_THETA = 10000.0
EPS = 1e-6

F32 = jnp.float32
BF16 = jnp.bfloat16

LANES = 128
TM = 512
NT_REAL = B * SEQ // TM
NT = NT_REAL + 1
ROWS = NT * TM
TILES_PER_SEQ = SEQ // TM
META_ROW0 = B * SEQ
TAB_ROWS = SEQ + TM
FC = 256
KB = 256
BQ = TM
NS = BQ // KB
HP = H
LOOKAHEAD = 5
HALO = 16
NEG = -1e30
VR = V_HEAD + 16
VMEM_LIMIT = 56 * 1024 * 1024
NT_DIMS = (((1,), (1,)), ((), ()))


def _rms(x, g):
    ms = jnp.mean(x * x, axis=-1, keepdims=True)
    return (x * lax.rsqrt(ms + EPS)) * g


def _params(n_axes=1):
    return pltpu.CompilerParams(
        dimension_semantics=("arbitrary",) * n_axes, vmem_limit_bytes=VMEM_LIMIT)


def _const_spec(shape):
    zeros = (0,) * len(shape)
    return pl.BlockSpec(shape, lambda i: zeros, pipeline_mode=pl.Buffered(1))


def _layer_spec(shape, layer):
    return pl.BlockSpec((None,) + shape, lambda i: (layer, 0, 0), pipeline_mode=pl.Buffered(1))


def _tab_tile(i):
    return jnp.where(i < NT_REAL, i % TILES_PER_SEQ, TILES_PER_SEQ)


def _pool_rows(x, halo, is_meta, g, w_ref, scale):
    u = _rms(x, g)
    uh = _rms(halo, g) * jnp.where(is_meta, 0.0, 1.0)
    pos = lax.broadcasted_iota(jnp.int32, (TM, POOL_GROUP), 0) + jnp.where(is_meta, 0, N_META)
    pad = jnp.zeros((HALO, POOL_GROUP), F32)
    outs = []
    for gi, w in enumerate(POOL_WINDOWS):
        sl = slice(gi * POOL_GROUP, (gi + 1) * POOL_GROUP)
        ug = u[:, sl]
        acc = jnp.concatenate([pad, uh[:, sl], ug], axis=0)
        step = 1
        while step < w:
            acc = acc + pltpu.roll(acc, step, 0)
            step *= 2
        cnt = jnp.minimum(pos + 1, w).astype(F32)
        pooled = acc[2 * HALO:, :] / cnt - ug
        y = jnp.dot(pooled.astype(BF16), w_ref[gi], preferred_element_type=F32)
        outs.append(x[:, sl] + y * scale[:, sl])
    return jnp.concatenate(outs, axis=1)


def _q_rows(y, g, wdq_ref, gl, wqt_ref, wqrt_ref, cos_t, sin_t, q_ref):
    u = _rms(y, g).astype(BF16)
    cq = jnp.dot(u, wdq_ref[...], preferred_element_type=F32)
    cqn = _rms(cq, gl).astype(BF16)
    qa = lax.dot_general(wqt_ref[...], cqn, NT_DIMS, preferred_element_type=F32)
    qb = lax.dot_general(wqrt_ref[...], cqn, NT_DIMS, preferred_element_type=F32)
    scale = (1.0 / math.sqrt(QK_NOPE + QK_ROPE)) * math.log2(math.e)
    lo, hi = QK_NOPE, QK_NOPE + QK_ROPE
    for hh in range(H):
        a = qa[hh * LANES:(hh + 1) * LANES, :]
        rope = a[lo:hi, :] * cos_t + qb[hh * QK_ROPE:(hh + 1) * QK_ROPE, :] * sin_t
        q_ref[hh] = (jnp.concatenate([a[:lo, :], rope, a[hi:, :]], axis=0) * scale).astype(BF16)


def _kv_rows(y, g, wd_ref, gl, wuk_ref, wuvt_ref, cos, sin, k_ref, vt_ref):
    xn = _rms(y, g).astype(BF16)
    ckr = jnp.dot(xn, wd_ref[...], preferred_element_type=F32)
    ckv = _rms(ckr[:, :KV_RANK], gl).astype(BF16)
    kr = ckr[:, KV_RANK:KV_RANK + LANES] * cos + ckr[:, KV_RANK + LANES:] * sin
    kn = jnp.dot(ckv, wuk_ref[...], preferred_element_type=F32)
    for hh in range(H):
        k_ref[hh] = (kn[:, hh * LANES:(hh + 1) * LANES] + kr).astype(BF16)
    vt = lax.dot_general(wuvt_ref[...], ckv, NT_DIMS, preferred_element_type=F32)
    ones_row = (lax.broadcasted_iota(jnp.int32, (VR - V_HEAD, TM), 0) == 0).astype(BF16)
    for hh in range(H):
        vt_ref[hh * VR:hh * VR + V_HEAD, :] = vt[hh * V_HEAD:(hh + 1) * V_HEAD, :].astype(BF16)
        vt_ref[hh * VR + V_HEAD:(hh + 1) * VR, :] = ones_row


def _layer_kernel(*refs, pre, post):
    refs = list(refs)

    def take(n):
        return [refs.pop(0) for _ in range(n)]

    i = pl.program_id(0)
    is_meta = i == NT - 1
    (x_ref,) = take(1)
    x = x_ref[...]
    if pre == "first":
        (xm_ref,) = take(1)
        x = jnp.where(is_meta, xm_ref[...], x)
    elif pre == "pool":
        halo_ref, pg_ref, pw_ref, ps_ref = take(4)
        x = _pool_rows(x, halo_ref[...], is_meta, pg_ref[...], pw_ref, ps_ref[...])
    elif pre == "wo":
        ot_ref, wo_ref = take(2)
        x = x + lax.dot_general(ot_ref[...], wo_ref[...], (((0,), (0,)), ((), ())),
                                preferred_element_type=F32)
    g_ref, wg_ref, wu_ref, wd_ref = take(4)
    post_in = take({"plain": 0, "final": 1, "q": 7, "kv": 7}[post])
    (o_ref,) = take(1)
    post_out = take({"plain": 0, "final": 0, "q": 1, "kv": 2}[post])
    xn_ref, a_ref = take(2)
    assert not refs

    xn_ref[...] = _rms(x, g_ref[...]).astype(BF16)
    for c in range(D_FF // FC):
        sl = slice(c * FC, (c + 1) * FC)
        gate = jnp.dot(xn_ref[...], wg_ref[:, sl], preferred_element_type=F32)
        up = jnp.dot(xn_ref[...], wu_ref[:, sl], preferred_element_type=F32)
        a_ref[:, sl] = (gate * jax.nn.sigmoid(gate) * up).astype(BF16)
    y = x + 0.5 * jnp.dot(a_ref[...], wd_ref[...], preferred_element_type=F32)

    if post == "final":
        y = _rms(y, post_in[0][...])
    o_ref[...] = y
    if post == "q":
        qg, wdq, qgl, wqt, wqrt, cos_t, sin_t = post_in
        _q_rows(y, qg[...], wdq, qgl[...], wqt, wqrt, cos_t[...], sin_t[...], post_out[0])
    elif post == "kv":
        kg, wdkv, kgl, wuk, wuvt, cos, sin = post_in
        _kv_rows(y, kg[...], wdkv, kgl[...], wuk, wuvt, cos[...], sin[...], *post_out)


def _layer(h, g, ffn_w, layer, pre="plain", pre_args=(), post="plain", post_args=()):
    n_tiles = NT_REAL if post == "final" else NT
    row_spec = pl.BlockSpec((TM, D), lambda i: (i, 0))
    vec = lambda v: v.reshape(1, -1)

    args, in_specs = [h], [row_spec]
    if pre == "first":
        in_specs[0] = pl.BlockSpec((TM, D), lambda i: (jnp.minimum(i, NT_REAL - 1), 0))
        args += list(pre_args)
        in_specs += [_const_spec((TM, D))]
    elif pre == "pool":
        def halo_map(i):
            first = (i % TILES_PER_SEQ) == 0
            return (jnp.where(first, META_ROW0 // HALO, i * (TM // HALO) - 1), 0)
        pg, pw, ps = pre_args
        args += [h, vec(pg), pw, vec(ps)]
        in_specs += [pl.BlockSpec((HALO, D), halo_map), _const_spec((1, D)),
                     _const_spec((len(POOL_WINDOWS), POOL_GROUP, POOL_GROUP)), _const_spec((1, D))]
    elif pre == "wo":
        args += list(pre_args)
        in_specs += [pl.BlockSpec((H * V_HEAD, TM), lambda i: (0, i)), _const_spec((H * V_HEAD, D))]

    args += [vec(g), *ffn_w]
    in_specs += [_const_spec((1, D)), _layer_spec((D, D_FF), layer), _layer_spec((D, D_FF), layer),
                 _layer_spec((D_FF, D), layer)]

    out_shape = [jax.ShapeDtypeStruct((n_tiles * TM, D), F32)]
    out_specs = [row_spec]
    if post == "final":
        args += [vec(post_args[0])]
        in_specs += [_const_spec((1, D))]
    elif post == "q":
        qg, wdq, qgl, wqt, wqrt, cos_t, sin_t = post_args
        tab_spec = pl.BlockSpec((QK_ROPE, TM), lambda i: (0, _tab_tile(i)))
        args += [vec(qg), wdq, vec(qgl), wqt, wqrt, cos_t, sin_t]
        in_specs += [_const_spec((1, D)), _const_spec((D, Q_RANK)), _const_spec((1, Q_RANK)),
                     _const_spec((H * LANES, Q_RANK)), _const_spec((H * QK_ROPE, Q_RANK)),
                     tab_spec, tab_spec]
        out_shape += [jax.ShapeDtypeStruct((H, LANES, ROWS), BF16)]
        out_specs += [pl.BlockSpec((H, LANES, TM), lambda i: (0, 0, i))]
    elif post == "kv":
        kg, wdkv, kgl, wuk, wuvt, cos, sin = post_args
        tab_spec = pl.BlockSpec((TM, LANES), lambda i: (_tab_tile(i), 0))
        args += [vec(kg), wdkv, vec(kgl), wuk, wuvt, cos, sin]
        in_specs += [_const_spec((1, D)), _const_spec((D, KV_RANK + 2 * LANES)),
                     _const_spec((1, KV_RANK)), _const_spec((KV_RANK, H * LANES)),
                     _const_spec((H * V_HEAD, KV_RANK)), tab_spec, tab_spec]
        out_shape += [jax.ShapeDtypeStruct((H, ROWS, LANES), BF16),
                      jax.ShapeDtypeStruct((H * VR, ROWS), BF16)]
        out_specs += [pl.BlockSpec((H, TM, LANES), lambda i: (0, i, 0)),
                      pl.BlockSpec((H * VR, TM), lambda i: (0, i))]

    outs = pl.pallas_call(
        functools.partial(_layer_kernel, pre=pre, post=post),
        grid=(n_tiles,),
        in_specs=in_specs,
        out_specs=out_specs,
        out_shape=out_shape,
        scratch_shapes=[pltpu.VMEM((TM, D), BF16), pltpu.VMEM((TM, D_FF), BF16)],
        compiler_params=_params(),
        name=f"layer_{pre}_{post}",
    )(*args)
    return outs[0] if len(outs) == 1 else outs


def _attn_kernel(qt_ref, k_ref, vt_ref, km_ref, vmt_ref, o_ref, m_ref, acc_ref, s_ref):
    j = pl.program_id(1)
    is_real = j < NT_REAL

    def head_rows(hh):
        return slice(hh * VR, (hh + 1) * VR)

    for hh in range(HP):
        s = jnp.dot(km_ref[hh], qt_ref[hh], preferred_element_type=F32)
        kidx = lax.broadcasted_iota(jnp.int32, s.shape, 0)
        s = jnp.where(kidx < N_META, s, NEG)
        m0 = jnp.max(s, axis=0, keepdims=True)
        p = jnp.exp2((s - m0).astype(BF16))
        m_ref[hh] = m0
        acc_ref[head_rows(hh), :] = jnp.dot(vmt_ref[head_rows(hh), :], p,
                                            preferred_element_type=F32)

    def qk(hh, c, kb):
        off = pl.multiple_of(kb * KB, KB)
        return jnp.dot(k_ref[hh, pl.ds(off, KB), :], qt_ref[hh, :, c * KB:(c + 1) * KB],
                       preferred_element_type=F32)

    def run_tiles(tiles, kb0, tail):
        chains = sorted({(hh, c) for hh, c, _, _ in tiles})
        state = {}
        for hh, c in chains:
            cs = slice(c * KB, (c + 1) * KB)
            state[hh, c] = (m_ref[hh, :, cs], acc_ref[head_rows(hh), cs])
        scores = {}
        pending = [(hh, c, kb0 + u) for hh, c, u, _ in tiles[LOOKAHEAD:]] + list(tail)
        n_issued = LOOKAHEAD
        for t, (hh, c, u, masked) in enumerate(tiles):
            m_old, acc = state[hh, c]
            s = s_ref[t] if t < LOOKAHEAD else scores.pop(t)
            if masked:
                s = jnp.where(chunk_mask, s, NEG)
            m_new = jnp.maximum(m_old, jnp.max(s, axis=0, keepdims=True))
            alpha = jnp.exp2(m_old - m_new)
            p = jnp.exp2((s - m_new).astype(BF16))
            if pending:
                nxt = qk(*pending.pop(0))
                if n_issued < len(tiles):
                    scores[n_issued] = nxt
                else:
                    s_ref[n_issued - len(tiles)] = nxt
                n_issued += 1
            off = pl.multiple_of((kb0 + u) * KB, KB)
            acc = alpha * acc + jnp.dot(vt_ref[head_rows(hh), pl.ds(off, KB)], p,
                                        preferred_element_type=F32)
            state[hh, c] = (m_new, acc)
        for hh, c in chains:
            cs = slice(c * KB, (c + 1) * KB)
            m_ref[hh, :, cs], acc_ref[head_rows(hh), cs] = state[hh, c]

    kpos = lax.broadcasted_iota(jnp.int32, (KB, KB), 0)
    qpos = lax.broadcasted_iota(jnp.int32, (KB, KB), 1)
    chunk_mask = (kpos // CHUNK) <= (qpos // CHUNK)

    full_tiles = [(hh, c, u, False) for u in range(NS) for hh in range(HP) for c in range(NS)]
    diag_tiles = [(hh, c, d, c == d) for d in range(NS) for hh in range(HP) for c in range(d, NS)]
    assert [t[:3] for t in full_tiles[:LOOKAHEAD]] == [t[:3] for t in diag_tiles[:LOOKAHEAD]]

    def first_scores(kb0):
        return [(hh, c, kb0 + u) for hh, c, u, _ in full_tiles[:LOOKAHEAD]]

    q_tile = j % TILES_PER_SEQ
    n_iter = jnp.where(is_real, q_tile, 0)

    def full_body(it, carry):
        run_tiles(full_tiles, it * NS, first_scores((it + 1) * NS))
        return carry

    for t, a in enumerate(first_scores(0)):
        s_ref[t] = qk(*a)
    lax.fori_loop(0, n_iter, full_body, 0)

    @pl.when(is_real)
    def _():
        run_tiles(diag_tiles, q_tile * NS, [])

    for hh in range(HP):
        acc = acc_ref[head_rows(hh), :]
        o_ref[hh * V_HEAD:(hh + 1) * V_HEAD, :] = (
            acc[:V_HEAD, :] / acc[V_HEAD:V_HEAD + 1, :]).astype(BF16)


def _attn(qt_all, k_all, vt_all):
    def b_of(j):
        return jnp.minimum(j // TILES_PER_SEQ, B - 1)

    return pl.pallas_call(
        _attn_kernel,
        grid=(H // HP, NT),
        in_specs=[pl.BlockSpec((HP, LANES, BQ), lambda h, j: (h, 0, j)),
                  pl.BlockSpec((HP, SEQ, LANES), lambda h, j: (h, b_of(j), 0),
                               pipeline_mode=pl.Buffered(1)),
                  pl.BlockSpec((HP * VR, SEQ), lambda h, j: (h, b_of(j)),
                               pipeline_mode=pl.Buffered(1)),
                  pl.BlockSpec((HP, LANES, LANES), lambda h, j: (h, META_ROW0 // LANES, 0)),
                  pl.BlockSpec((HP * VR, LANES), lambda h, j: (h, META_ROW0 // LANES))],
        out_specs=pl.BlockSpec((HP * V_HEAD, BQ), lambda h, j: (h, j)),
        out_shape=jax.ShapeDtypeStruct((H * V_HEAD, ROWS), BF16),
        scratch_shapes=[pltpu.VMEM((HP, 1, BQ), F32), pltpu.VMEM((HP * VR, BQ), F32),
                        pltpu.VMEM((LOOKAHEAD, KB, KB), F32)],
        compiler_params=_params(2),
        name="attn",
    )(qt_all, k_all, vt_all, k_all, vt_all)


def _rot_cols(w):
    half = QK_ROPE // 2
    return jnp.concatenate([-w[..., half:], w[..., :half]], axis=-1)


def _place_rope(w):
    pad = [(0, 0)] * (w.ndim - 1) + [(QK_NOPE, LANES - QK_NOPE - QK_ROPE)]
    return jnp.pad(w, pad)


def _rope_tables():
    inv = 1.0 / (ROPE_THETA ** (jnp.arange(0, QK_ROPE, 2, dtype=F32) / QK_ROPE))
    row = jnp.arange(TAB_ROWS)
    pos = jnp.where(row < SEQ, N_META + row, row - SEQ).astype(F32)
    ang = pos[:, None] * inv[None, :]
    ones = jnp.ones((TAB_ROWS, QK_NOPE), F32)
    ones_hi = jnp.ones((TAB_ROWS, LANES - QK_NOPE - QK_ROPE), F32)
    cos_t = jnp.concatenate([ones, jnp.cos(ang), jnp.cos(ang), ones_hi], axis=1)
    sin_t = jnp.concatenate([0 * ones, jnp.sin(ang), jnp.sin(ang), 0 * ones_hi], axis=1)
    return cos_t, sin_t


def kernel(x, meta_tokens, ffn1_norm, ffn1_w_gate, ffn1_w_up, ffn1_w_down, mix_norm, ffn2_norm, ffn2_w_gate, ffn2_w_up, ffn2_w_down, pool_w, pool_scale, kv_in_norm, w_dkv, kv_latent_norm, w_uk, w_uv, w_dq, q_latent_norm, w_uq, w_o, final_norm):
    cos_t, sin_t = _rope_tables()
    frames = x.reshape(B * SEQ, D)
    meta_tile = jnp.concatenate(
        [meta_tokens.astype(x.dtype), jnp.zeros((TM - N_META, D), x.dtype)], axis=0)
    ffn1_w = [w.astype(BF16) for w in (ffn1_w_gate, ffn1_w_up, ffn1_w_down)]
    ffn2_w = [w.astype(BF16) for w in (ffn2_w_gate, ffn2_w_up, ffn2_w_down)]

    w_kr = w_dkv[:, KV_RANK:]
    wd_kv = jnp.concatenate(
        [w_dkv[:, :KV_RANK], _place_rope(w_kr), _place_rope(_rot_cols(w_kr))], axis=1).astype(BF16)
    wuk = jnp.pad(w_uk.reshape(KV_RANK, H, QK_NOPE),
                  ((0, 0), (0, 0), (0, LANES - QK_NOPE))).reshape(KV_RANK, H * LANES).astype(BF16)
    kv_args = (kv_in_norm, wd_kv, kv_latent_norm, wuk, w_uv.T.astype(BF16), cos_t, sin_t)

    def q_args(l):
        jj = l - N_A
        wq3 = w_uq[jj].reshape(Q_RANK, H, QK_NOPE + QK_ROPE)
        wq = jnp.pad(wq3, ((0, 0), (0, 0), (0, LANES - QK_NOPE - QK_ROPE)))
        wqr = _rot_cols(wq3[:, :, QK_NOPE:])
        to_t = lambda w: w.reshape(Q_RANK, -1).T.astype(BF16)
        rope_rows = slice(QK_NOPE, QK_NOPE + QK_ROPE)
        return (mix_norm[l], w_dq[jj].astype(BF16), q_latent_norm[jj], to_t(wq), to_t(wqr),
                cos_t[:, rope_rows].T, sin_t[:, rope_rows].T)

    h = _layer(frames, ffn1_norm[0], ffn1_w, 0, pre="first", pre_args=(meta_tile,))
    h = _layer(h, ffn2_norm[0], ffn2_w, 0, pre="pool",
               pre_args=(mix_norm[0], pool_w[0].astype(BF16), pool_scale[0]))
    h = _layer(h, ffn1_norm[1], ffn1_w, 1)
    h, k_all, vt_all = _layer(h, ffn2_norm[1], ffn2_w, 1, pre="pool",
                              pre_args=(mix_norm[1], pool_w[1].astype(BF16), pool_scale[1]),
                              post="kv", post_args=kv_args)
    for l in range(N_A, DEPTH):
        h, qt_all = _layer(h, ffn1_norm[l], ffn1_w, l, post="q", post_args=q_args(l))
        ot = _attn(qt_all, k_all, vt_all)
        last = l == DEPTH - 1
        h = _layer(h, ffn2_norm[l], ffn2_w, l, pre="wo", pre_args=(ot, w_o[l - N_A].astype(BF16)),
                   post="final" if last else "plain", post_args=(final_norm,) if last else ())
    return h.reshape(B, SEQ, D)
```

```python
import functools
import math

import jax
import jax.numpy as jnp
from jax import lax
from jax.experimental import pallas as pl
from jax.experimental.pallas import tpu as pltpu

D = 1024
B = 2
SEQ = 8192
DEPTH = 4
CHUNK = 64
N_META = 16
N_A = DEPTH // 2
D_FF = 2816
POOL_WINDOWS = (2, 4, 8, 16)
POOL_GROUP = D // len(POOL_WINDOWS)
H = 8
QK_NOPE = 64
QK_ROPE = 32
V_HEAD = 64
KV_RANK = 256
Q_RANK = 384
ROPE_THETA = 10000.0
EPS = 1e-6

F32 = jnp.float32
BF16 = jnp.bfloat16

LANES = 128
TM = 512
NT_REAL = B * SEQ // TM
NT = NT_REAL + 1
ROWS = NT * TM
TILES_PER_SEQ = SEQ // TM
META_ROW0 = B * SEQ
TAB_ROWS = SEQ + TM
FC = 256
KB = 256
BQ = TM
NS = BQ // KB
HP = H
LOOKAHEAD = 6
HALO = 16
NEG = -1e30
META = -1
VR = V_HEAD + 16
VMEM_LIMIT = 56 * 1024 * 1024
NT_DIMS = (((1,), (1,)), ((), ()))


def _rms(x, g):
    ms = jnp.mean(x * x, axis=-1, keepdims=True)
    return (x * lax.rsqrt(ms + EPS)) * g


def _params(n_axes=1):
    return pltpu.CompilerParams(
        dimension_semantics=("arbitrary",) * n_axes, vmem_limit_bytes=VMEM_LIMIT)


def _const_spec(shape):
    zeros = (0,) * len(shape)
    return pl.BlockSpec(shape, lambda i: zeros, pipeline_mode=pl.Buffered(1))


def _layer_spec(shape, layer):
    return pl.BlockSpec((None,) + shape, lambda i: (layer, 0, 0), pipeline_mode=pl.Buffered(1))


def _tab_tile(i):
    return jnp.where(i < NT_REAL, i % TILES_PER_SEQ, TILES_PER_SEQ)


def _pool_rows(x, halo, is_meta, g, w_ref, scale):
    u = _rms(x, g)
    uh = _rms(halo, g) * jnp.where(is_meta, 0.0, 1.0)
    n_avail = lax.broadcasted_iota(jnp.int32, (HALO, POOL_GROUP), 0) + jnp.where(is_meta, 1, HALO)
    pad = jnp.zeros((HALO, POOL_GROUP), F32)
    outs = []
    for gi, w in enumerate(POOL_WINDOWS):
        sl = slice(gi * POOL_GROUP, (gi + 1) * POOL_GROUP)
        ug = u[:, sl]
        acc = jnp.concatenate([pad, uh[:, sl], ug], axis=0)
        step = 1
        while step < w:
            acc = acc + pltpu.roll(acc, step, 0)
            step *= 2
        head = acc[2 * HALO:3 * HALO, :] / jnp.minimum(n_avail, w).astype(F32)
        pooled = jnp.concatenate([head, acc[3 * HALO:, :] * (1.0 / w)], axis=0) - ug
        y = jnp.dot(pooled.astype(BF16), w_ref[gi], preferred_element_type=F32)
        outs.append(x[:, sl] + y * scale[:, sl])
    return jnp.concatenate(outs, axis=1)


def _q_rows(y, g, wdq_ref, gl, wqt_ref, wqrt_ref, cos_t, sin_t, q_ref):
    u = _rms(y, g).astype(BF16)
    cq = jnp.dot(u, wdq_ref[...], preferred_element_type=F32)
    cqn = _rms(cq, gl).astype(BF16)
    qa = lax.dot_general(wqt_ref[...], cqn, NT_DIMS, preferred_element_type=F32)
    qb = lax.dot_general(wqrt_ref[...], cqn, NT_DIMS, preferred_element_type=F32)
    scale = (1.0 / math.sqrt(QK_NOPE + QK_ROPE)) * math.log2(math.e)
    lo, hi = QK_NOPE, QK_NOPE + QK_ROPE
    for hh in range(H):
        a = qa[hh * LANES:(hh + 1) * LANES, :]
        rope = a[lo:hi, :] * cos_t + qb[hh * QK_ROPE:(hh + 1) * QK_ROPE, :] * sin_t
        q_ref[hh] = (jnp.concatenate([a[:lo, :], rope, a[hi:, :]], axis=0) * scale).astype(BF16)


def _kv_rows(y, g, wd_ref, gl, wuk_ref, wuvt_ref, cos, sin, k_ref, vt_ref):
    xn = _rms(y, g).astype(BF16)
    ckr = jnp.dot(xn, wd_ref[...], preferred_element_type=F32)
    ckv = _rms(ckr[:, :KV_RANK], gl).astype(BF16)
    kr = ckr[:, KV_RANK:KV_RANK + LANES] * cos + ckr[:, KV_RANK + LANES:] * sin
    kn = jnp.dot(ckv, wuk_ref[...], preferred_element_type=F32)
    for hh in range(H):
        k_ref[hh] = (kn[:, hh * LANES:(hh + 1) * LANES] + kr).astype(BF16)
    vt = lax.dot_general(wuvt_ref[...], ckv, NT_DIMS, preferred_element_type=F32)
    ones_row = (lax.broadcasted_iota(jnp.int32, (VR - V_HEAD, TM), 0) == 0).astype(BF16)
    for hh in range(H):
        vt_ref[hh * VR:hh * VR + V_HEAD, :] = vt[hh * V_HEAD:(hh + 1) * V_HEAD, :].astype(BF16)
        vt_ref[hh * VR + V_HEAD:(hh + 1) * VR, :] = ones_row


def _layer_kernel(*refs, pre, post):
    refs = list(refs)

    def take(n):
        return [refs.pop(0) for _ in range(n)]

    i = pl.program_id(0)
    is_meta = i == NT - 1
    (x_ref,) = take(1)
    x = x_ref[...]
    if pre == "first":
        (xm_ref,) = take(1)
        x = jnp.where(is_meta, xm_ref[...], x)
    elif pre == "pool":
        halo_ref, pg_ref, pw_ref, ps_ref = take(4)
        x = _pool_rows(x, halo_ref[...], is_meta, pg_ref[...], pw_ref, ps_ref[...])
    elif pre == "wo":
        ot_ref, wo_ref = take(2)
        x = x + lax.dot_general(ot_ref[...], wo_ref[...], (((0,), (0,)), ((), ())),
                                preferred_element_type=F32)
    g_ref, wg_ref, wu_ref, wd_ref = take(4)
    post_in = take({"plain": 0, "final": 1, "q": 7, "kv": 7}[post])
    (o_ref,) = take(1)
    post_out = take({"plain": 0, "final": 0, "q": 1, "kv": 2}[post])
    xn_ref, a_ref = take(2)
    assert not refs

    xn_ref[...] = _rms(x, g_ref[...]).astype(BF16)
    for c in range(D_FF // FC):
        sl = slice(c * FC, (c + 1) * FC)
        gate = jnp.dot(xn_ref[...], wg_ref[:, sl].astype(BF16), preferred_element_type=F32)
        up = jnp.dot(xn_ref[...], wu_ref[:, sl].astype(BF16), preferred_element_type=F32)
        a_ref[:, sl] = (gate * jax.nn.sigmoid(gate) * up).astype(BF16)
    y = jnp.concatenate(
        [x[:, n * FC:(n + 1) * FC] + 0.5 * jnp.dot(
            a_ref[...], wd_ref[:, n * FC:(n + 1) * FC].astype(BF16), preferred_element_type=F32)
         for n in range(D // FC)], axis=1)

    if post == "final":
        y = _rms(y, post_in[0][...])
    o_ref[...] = y
    if post == "q":
        qg, wdq, qgl, wqt, wqrt, cos_t, sin_t = post_in
        _q_rows(y, qg[...], wdq, qgl[...], wqt, wqrt, cos_t[...], sin_t[...], post_out[0])
    elif post == "kv":
        kg, wdkv, kgl, wuk, wuvt, cos, sin = post_in
        _kv_rows(y, kg[...], wdkv, kgl[...], wuk, wuvt, cos[...], sin[...], *post_out)


def _layer(h, g, ffn_w, layer, pre="plain", pre_args=(), post="plain", post_args=()):
    n_tiles = NT_REAL if post == "final" else NT
    row_spec = pl.BlockSpec((TM, D), lambda i: (i, 0))
    vec = lambda v: v.reshape(1, -1)

    args, in_specs = [h], [row_spec]
    if pre == "first":
        in_specs[0] = pl.BlockSpec((TM, D), lambda i: (jnp.minimum(i, NT_REAL - 1), 0))
        args += list(pre_args)
        in_specs += [_const_spec((TM, D))]
    elif pre == "pool":
        def halo_map(i):
            first = (i % TILES_PER_SEQ) == 0
            return (jnp.where(first, META_ROW0 // HALO, i * (TM // HALO) - 1), 0)
        pg, pw, ps = pre_args
        args += [h, vec(pg), pw, vec(ps)]
        in_specs += [pl.BlockSpec((HALO, D), halo_map), _const_spec((1, D)),
                     _const_spec((len(POOL_WINDOWS), POOL_GROUP, POOL_GROUP)), _const_spec((1, D))]
    elif pre == "wo":
        args += list(pre_args)
        in_specs += [pl.BlockSpec((H * V_HEAD, TM), lambda i: (0, i)), _const_spec((H * V_HEAD, D))]

    args += [vec(g), *ffn_w]
    in_specs += [_const_spec((1, D)), _layer_spec((D, D_FF), layer), _layer_spec((D, D_FF), layer),
                 _layer_spec((D_FF, D), layer)]

    out_shape = [jax.ShapeDtypeStruct((n_tiles * TM, D), F32)]
    out_specs = [row_spec]
    if post == "final":
        args += [vec(post_args[0])]
        in_specs += [_const_spec((1, D))]
    elif post == "q":
        qg, wdq, qgl, wqt, wqrt, cos_t, sin_t = post_args
        tab_spec = pl.BlockSpec((QK_ROPE, TM), lambda i: (0, _tab_tile(i)))
        args += [vec(qg), wdq, vec(qgl), wqt, wqrt, cos_t, sin_t]
        in_specs += [_const_spec((1, D)), _const_spec((D, Q_RANK)), _const_spec((1, Q_RANK)),
                     _const_spec((H * LANES, Q_RANK)), _const_spec((H * QK_ROPE, Q_RANK)),
                     tab_spec, tab_spec]
        out_shape += [jax.ShapeDtypeStruct((H, LANES, ROWS), BF16)]
        out_specs += [pl.BlockSpec((H, LANES, TM), lambda i: (0, 0, i))]
    elif post == "kv":
        kg, wdkv, kgl, wuk, wuvt, cos, sin = post_args
        tab_spec = pl.BlockSpec((TM, LANES), lambda i: (_tab_tile(i), 0))
        args += [vec(kg), wdkv, vec(kgl), wuk, wuvt, cos, sin]
        in_specs += [_const_spec((1, D)), _const_spec((D, KV_RANK + 2 * LANES)),
                     _const_spec((1, KV_RANK)), _const_spec((KV_RANK, H * LANES)),
                     _const_spec((H * V_HEAD, KV_RANK)), tab_spec, tab_spec]
        out_shape += [jax.ShapeDtypeStruct((H, ROWS, LANES), BF16),
                      jax.ShapeDtypeStruct((H * VR, ROWS), BF16)]
        out_specs += [pl.BlockSpec((H, TM, LANES), lambda i: (0, i, 0)),
                      pl.BlockSpec((H * VR, TM), lambda i: (0, i))]

    outs = pl.pallas_call(
        functools.partial(_layer_kernel, pre=pre, post=post),
        grid=(n_tiles,),
        in_specs=in_specs,
        out_specs=out_specs,
        out_shape=out_shape,
        scratch_shapes=[pltpu.VMEM((TM, D), BF16), pltpu.VMEM((TM, D_FF), BF16)],
        compiler_params=_params(),
        name=f"layer_{pre}_{post}",
    )(*args)
    return outs[0] if len(outs) == 1 else outs


def _attn_kernel(qt_ref, k_ref, vt_ref, km_ref, vmt_ref, o_ref, m_ref, acc_ref, s_ref):
    j = pl.program_id(1)
    is_real = j < NT_REAL

    def head_rows(hh):
        return slice(hh * VR, (hh + 1) * VR)

    def keys_of(hh, u, kb0):
        if u == META:
            return km_ref[hh], vmt_ref[head_rows(hh), :]
        off = pl.multiple_of((kb0 + u) * KB, KB)
        return k_ref[hh, pl.ds(off, KB), :], vt_ref[head_rows(hh), pl.ds(off, KB)]

    def qk(hh, c, u, kb0):
        return jnp.dot(keys_of(hh, u, kb0)[0], qt_ref[hh, :, c * KB:(c + 1) * KB],
                       preferred_element_type=F32)

    kpos = lax.broadcasted_iota(jnp.int32, (KB, KB), 0)
    qpos = lax.broadcasted_iota(jnp.int32, (KB, KB), 1)
    chunk_mask = (kpos // CHUNK) <= (qpos // CHUNK)
    meta_mask = lax.broadcasted_iota(jnp.int32, (LANES, KB), 0) < N_META

    def run_tiles(tiles, kb0, carried, tail):
        chains = sorted({(hh, c) for hh, c, _, _ in tiles})
        state = {}
        for hh, c in chains:
            cs = slice(c * KB, (c + 1) * KB)
            state[hh, c] = (m_ref[hh, :, cs], acc_ref[head_rows(hh), cs])
        scores = {}
        pending = [(t, (hh, c, u, kb0)) for t, (hh, c, u, _) in enumerate(tiles)]
        pending = pending[LOOKAHEAD if carried else 0:]
        pending += [(len(tiles) + i, a) for i, a in enumerate(tail)]

        def issue():
            if pending:
                t, a = pending.pop(0)
                if t < len(tiles):
                    scores[t] = qk(*a)
                else:
                    s_ref[t - len(tiles)] = qk(*a)

        if not carried:
            for _ in range(LOOKAHEAD):
                issue()
        for t, (hh, c, u, masked) in enumerate(tiles):
            m_old, acc = state[hh, c]
            s = s_ref[t] if (carried and t < LOOKAHEAD) else scores.pop(t)
            if u == META:
                s = jnp.where(meta_mask, s, NEG)
            elif masked:
                s = jnp.where(chunk_mask, s, NEG)
            m_new = jnp.maximum(m_old, jnp.max(s, axis=0, keepdims=True))
            alpha = jnp.exp2(m_old - m_new)
            p = jnp.exp2((s - m_new).astype(BF16))
            issue()
            acc = alpha * acc + jnp.dot(keys_of(hh, u, kb0)[1], p, preferred_element_type=F32)
            state[hh, c] = (m_new, acc)
        for hh, c in chains:
            cs = slice(c * KB, (c + 1) * KB)
            m_ref[hh, :, cs], acc_ref[head_rows(hh), cs] = state[hh, c]

    full_tiles = [(hh, c, u, False) for u in range(NS) for hh in range(HP) for c in range(NS)]
    diag_tiles = [(hh, c, d, c == d) for d in range(NS) for hh in range(HP) for c in range(d, NS)]
    meta_tiles = [(hh, c, META, True) for hh in range(HP) for c in range(NS)]
    assert [t[:3] for t in full_tiles[:LOOKAHEAD]] == [t[:3] for t in diag_tiles[:LOOKAHEAD]]

    def first_scores(kb0):
        return [(hh, c, u, kb0) for hh, c, u, _ in full_tiles[:LOOKAHEAD]]

    q_tile = j % TILES_PER_SEQ
    n_iter = jnp.where(is_real, q_tile, 0)

    m_ref[...] = jnp.full(m_ref.shape, NEG, F32)
    acc_ref[...] = jnp.zeros(acc_ref.shape, F32)
    for t, a in enumerate(first_scores(0)):
        s_ref[t] = qk(*a)

    def full_body(it, carry):
        run_tiles(full_tiles, it * NS, True, first_scores((it + 1) * NS))
        return carry

    lax.fori_loop(0, n_iter, full_body, 0)

    @pl.when(is_real)
    def _():
        run_tiles(diag_tiles + meta_tiles, q_tile * NS, True, [])

    @pl.when(jnp.logical_not(is_real))
    def _():
        run_tiles(meta_tiles, 0, False, [])

    for hh in range(HP):
        acc = acc_ref[head_rows(hh), :]
        o_ref[hh * V_HEAD:(hh + 1) * V_HEAD, :] = (
            acc[:V_HEAD, :] / acc[V_HEAD:V_HEAD + 1, :]).astype(BF16)


def _attn(qt_all, k_all, vt_all):
    def b_of(j):
        return jnp.minimum(j // TILES_PER_SEQ, B - 1)

    return pl.pallas_call(
        _attn_kernel,
        grid=(H // HP, NT),
        in_specs=[pl.BlockSpec((HP, LANES, BQ), lambda h, j: (h, 0, j)),
                  pl.BlockSpec((HP, SEQ, LANES), lambda h, j: (h, b_of(j), 0),
                               pipeline_mode=pl.Buffered(1)),
                  pl.BlockSpec((HP * VR, SEQ), lambda h, j: (h, b_of(j)),
                               pipeline_mode=pl.Buffered(1)),
                  pl.BlockSpec((HP, LANES, LANES), lambda h, j: (h, META_ROW0 // LANES, 0)),
                  pl.BlockSpec((HP * VR, LANES), lambda h, j: (h, META_ROW0 // LANES))],
        out_specs=pl.BlockSpec((HP * V_HEAD, BQ), lambda h, j: (h, j)),
        out_shape=jax.ShapeDtypeStruct((H * V_HEAD, ROWS), BF16),
        scratch_shapes=[pltpu.VMEM((HP, 1, BQ), F32), pltpu.VMEM((HP * VR, BQ), F32),
                        pltpu.VMEM((LOOKAHEAD, KB, KB), F32)],
        compiler_params=_params(2),
        name="attn",
    )(qt_all, k_all, vt_all, k_all, vt_all)


def _rot_cols(w):
    half = QK_ROPE // 2
    return jnp.concatenate([-w[..., half:], w[..., :half]], axis=-1)


def _place_rope(w):
    pad = [(0, 0)] * (w.ndim - 1) + [(QK_NOPE, LANES - QK_NOPE - QK_ROPE)]
    return jnp.pad(w, pad)


def _rope_tables():
    inv = 1.0 / (ROPE_THETA ** (jnp.arange(0, QK_ROPE, 2, dtype=F32) / QK_ROPE))
    row = jnp.arange(TAB_ROWS)
    pos = jnp.where(row < SEQ, N_META + row, row - SEQ).astype(F32)
    ang = pos[:, None] * inv[None, :]
    ones = jnp.ones((TAB_ROWS, QK_NOPE), F32)
    ones_hi = jnp.ones((TAB_ROWS, LANES - QK_NOPE - QK_ROPE), F32)
    cos_t = jnp.concatenate([ones, jnp.cos(ang), jnp.cos(ang), ones_hi], axis=1)
    sin_t = jnp.concatenate([0 * ones, jnp.sin(ang), jnp.sin(ang), 0 * ones_hi], axis=1)
    return cos_t, sin_t


def kernel(x, meta_tokens, ffn1_norm, ffn1_w_gate, ffn1_w_up, ffn1_w_down, mix_norm, ffn2_norm, ffn2_w_gate, ffn2_w_up, ffn2_w_down, pool_w, pool_scale, kv_in_norm, w_dkv, kv_latent_norm, w_uk, w_uv, w_dq, q_latent_norm, w_uq, w_o, final_norm):
    cos_t, sin_t = _rope_tables()
    frames = x.reshape(B * SEQ, D)
    meta_tile = jnp.concatenate(
        [meta_tokens.astype(x.dtype), jnp.zeros((TM - N_META, D), x.dtype)], axis=0)
    ffn1_w = [ffn1_w_gate, ffn1_w_up, ffn1_w_down]
    ffn2_w = [ffn2_w_gate, ffn2_w_up, ffn2_w_down]

    w_kr = w_dkv[:, KV_RANK:]
    wd_kv = jnp.concatenate(
        [w_dkv[:, :KV_RANK], _place_rope(w_kr), _place_rope(_rot_cols(w_kr))], axis=1).astype(BF16)
    wuk = jnp.pad(w_uk.reshape(KV_RANK, H, QK_NOPE),
                  ((0, 0), (0, 0), (0, LANES - QK_NOPE))).reshape(KV_RANK, H * LANES).astype(BF16)
    kv_args = (kv_in_norm, wd_kv, kv_latent_norm, wuk, w_uv.T.astype(BF16), cos_t, sin_t)

    def q_args(l):
        jj = l - N_A
        wq3 = w_uq[jj].reshape(Q_RANK, H, QK_NOPE + QK_ROPE)
        wq = jnp.pad(wq3, ((0, 0), (0, 0), (0, LANES - QK_NOPE - QK_ROPE)))
        wqr = _rot_cols(wq3[:, :, QK_NOPE:])
        to_t = lambda w: w.reshape(Q_RANK, -1).T.astype(BF16)
        rope_rows = slice(QK_NOPE, QK_NOPE + QK_ROPE)
        return (mix_norm[l], w_dq[jj].astype(BF16), q_latent_norm[jj], to_t(wq), to_t(wqr),
                cos_t[:, rope_rows].T, sin_t[:, rope_rows].T)

    h = _layer(frames, ffn1_norm[0], ffn1_w, 0, pre="first", pre_args=(meta_tile,))
    h = _layer(h, ffn2_norm[0], ffn2_w, 0, pre="pool",
               pre_args=(mix_norm[0], pool_w[0].astype(BF16), pool_scale[0]))
    h = _layer(h, ffn1_norm[1], ffn1_w, 1)
    h, k_all, vt_all = _layer(h, ffn2_norm[1], ffn2_w, 1, pre="pool",
                              pre_args=(mix_norm[1], pool_w[1].astype(BF16), pool_scale[1]),
                              post="kv", post_args=kv_args)
    for l in range(N_A, DEPTH):
        h, qt_all = _layer(h, ffn1_norm[l], ffn1_w, l, post="q", post_args=q_args(l))
        ot = _attn(qt_all, k_all, vt_all)
        last = l == DEPTH - 1
        h = _layer(h, ffn2_norm[l], ffn2_w, l, pre="wo", pre_args=(ot, w_o[l - N_A].astype(BF16)),
                   post="final" if last else "plain", post_args=(final_norm,) if last else ())
    return h.reshape(B, SEQ, D)
```

```python
import functools
import math

import jax
import jax.numpy as jnp
from jax import lax
from jax.experimental import pallas as pl
from jax.experimental.pallas import tpu as pltpu

D = 1024
B = 2
SEQ = 8192
DEPTH = 4
CHUNK = 64
N_META = 16
N_A = DEPTH // 2
D_FF = 2816
POOL_WINDOWS = (2, 4, 8, 16)
POOL_GROUP = D // len(POOL_WINDOWS)
H = 8
QK_NOPE = 64
QK_ROPE = 32
V_HEAD = 64
KV_RANK = 256
Q_RANK = 384
ROPE_THETA = 10000.0
EPS = 1e-6

F32 = jnp.float32
BF16 = jnp.bfloat16

LANES = 128
TM = 512
NT_REAL = B * SEQ // TM
NT = NT_REAL + 1
ROWS = NT * TM
TILES_PER_SEQ = SEQ // TM
META_ROW0 = B * SEQ
TAB_ROWS = SEQ + TM
FC = 256
KB = 256
BQ = TM
NS = BQ // KB
HP = H
LOOKAHEAD = 6
HALO = 16
NEG = -1e30
META = -1
VR = V_HEAD + 16
VMEM_LIMIT = 56 * 1024 * 1024
NT_DIMS = (((1,), (1,)), ((), ()))


def _rms(x, g):
    ms = jnp.mean(x * x, axis=-1, keepdims=True)
    return (x * lax.rsqrt(ms + EPS)) * g


def _params(n_axes=1):
    return pltpu.CompilerParams(
        dimension_semantics=("arbitrary",) * n_axes, vmem_limit_bytes=VMEM_LIMIT)


def _const_spec(shape):
    zeros = (0,) * len(shape)
    return pl.BlockSpec(shape, lambda i: zeros, pipeline_mode=pl.Buffered(1))


def _layer_spec(shape, layer):
    return pl.BlockSpec((None,) + shape, lambda i: (layer, 0, 0), pipeline_mode=pl.Buffered(1))


def _tab_tile(i):
    return jnp.where(i < NT_REAL, i % TILES_PER_SEQ, TILES_PER_SEQ)


def _pool_rows(x, halo, is_meta, g, w_ref, scale):
    u = _rms(x, g)
    uh = _rms(halo, g) * jnp.where(is_meta, 0.0, 1.0)
    n_avail = lax.broadcasted_iota(jnp.int32, (HALO, POOL_GROUP), 0) + jnp.where(is_meta, 1, HALO)
    pad = jnp.zeros((HALO, POOL_GROUP), F32)
    outs = []
    for gi, w in enumerate(POOL_WINDOWS):
        sl = slice(gi * POOL_GROUP, (gi + 1) * POOL_GROUP)
        ug = u[:, sl]
        acc = jnp.concatenate([pad, uh[:, sl], ug], axis=0)
        step = 1
        while step < w:
            acc = acc + pltpu.roll(acc, step, 0)
            step *= 2
        head = acc[2 * HALO:3 * HALO, :] / jnp.minimum(n_avail, w).astype(F32)
        pooled = jnp.concatenate([head, acc[3 * HALO:, :] * (1.0 / w)], axis=0) - ug
        y = jnp.dot(pooled.astype(BF16), w_ref[gi], preferred_element_type=F32)
        outs.append(x[:, sl] + y * scale[:, sl])
    return jnp.concatenate(outs, axis=1)


def _q_rows(y, g, wdq_ref, gl, wqt_ref, wqrt_ref, cos_t, sin_t, q_ref):
    u = _rms(y, g).astype(BF16)
    cq = jnp.dot(u, wdq_ref[...], preferred_element_type=F32)
    cqn = _rms(cq, gl).astype(BF16)
    qa = lax.dot_general(wqt_ref[...], cqn, NT_DIMS, preferred_element_type=F32)
    qb = lax.dot_general(wqrt_ref[...], cqn, NT_DIMS, preferred_element_type=F32)
    scale = (1.0 / math.sqrt(QK_NOPE + QK_ROPE)) * math.log2(math.e)
    lo, hi = QK_NOPE, QK_NOPE + QK_ROPE
    for hh in range(H):
        a = qa[hh * LANES:(hh + 1) * LANES, :]
        rope = a[lo:hi, :] * cos_t + qb[hh * QK_ROPE:(hh + 1) * QK_ROPE, :] * sin_t
        q_ref[hh] = (jnp.concatenate([a[:lo, :], rope, a[hi:, :]], axis=0) * scale).astype(BF16)


def _kv_rows(y, g, wd_ref, gl, wuk_ref, wuvt_ref, cos, sin, k_ref, vt_ref):
    xn = _rms(y, g).astype(BF16)
    ckr = jnp.dot(xn, wd_ref[...], preferred_element_type=F32)
    ckv = _rms(ckr[:, :KV_RANK], gl).astype(BF16)
    kr = ckr[:, KV_RANK:KV_RANK + LANES] * cos + ckr[:, KV_RANK + LANES:] * sin
    kn = jnp.dot(ckv, wuk_ref[...], preferred_element_type=F32)
    for hh in range(H):
        k_ref[hh] = (kn[:, hh * LANES:(hh + 1) * LANES] + kr).astype(BF16)
    vt = lax.dot_general(wuvt_ref[...], ckv, NT_DIMS, preferred_element_type=F32)
    ones_row = (lax.broadcasted_iota(jnp.int32, (VR - V_HEAD, TM), 0) == 0).astype(BF16)
    for hh in range(H):
        vt_ref[hh * VR:hh * VR + V_HEAD, :] = vt[hh * V_HEAD:(hh + 1) * V_HEAD, :].astype(BF16)
        vt_ref[hh * VR + V_HEAD:(hh + 1) * VR, :] = ones_row


def _layer_kernel(*refs, pre, post):
    refs = list(refs)

    def take(n):
        return [refs.pop(0) for _ in range(n)]

    i = pl.program_id(0)
    is_meta = i == NT - 1
    (x_ref,) = take(1)
    x = x_ref[...]
    if pre == "first":
        (xm_ref,) = take(1)
        x = jnp.where(is_meta, xm_ref[...], x)
    elif pre == "pool":
        halo_ref, pg_ref, pw_ref, ps_ref = take(4)
        x = _pool_rows(x, halo_ref[...], is_meta, pg_ref[...], pw_ref, ps_ref[...])
    elif pre == "wo":
        ot_ref, wo_ref = take(2)
        x = x + lax.dot_general(ot_ref[...], wo_ref[...], (((0,), (0,)), ((), ())),
                                preferred_element_type=F32)
    g_ref, wg_ref, wu_ref, wd_ref = take(4)
    post_in = take({"plain": 0, "final": 1, "q": 7, "kv": 7}[post])
    (o_ref,) = take(1)
    post_out = take({"plain": 0, "final": 0, "q": 1, "kv": 2}[post])
    xn_ref, a_ref = take(2)
    assert not refs

    xn_ref[...] = _rms(x, g_ref[...]).astype(BF16)
    for c in range(D_FF // FC):
        sl = slice(c * FC, (c + 1) * FC)
        gate = jnp.dot(xn_ref[...], wg_ref[:, sl].astype(BF16), preferred_element_type=F32)
        up = jnp.dot(xn_ref[...], wu_ref[:, sl].astype(BF16), preferred_element_type=F32)
        a_ref[:, sl] = (gate * jax.nn.sigmoid(gate) * up).astype(BF16)
    y = jnp.concatenate(
        [x[:, n * FC:(n + 1) * FC] + 0.5 * jnp.dot(
            a_ref[...], wd_ref[:, n * FC:(n + 1) * FC].astype(BF16), preferred_element_type=F32)
         for n in range(D // FC)], axis=1)

    if post == "final":
        y = _rms(y, post_in[0][...])
    o_ref[...] = y
    if post == "q":
        qg, wdq, qgl, wqt, wqrt, cos_t, sin_t = post_in
        _q_rows(y, qg[...], wdq, qgl[...], wqt, wqrt, cos_t[...], sin_t[...], post_out[0])
    elif post == "kv":
        kg, wdkv, kgl, wuk, wuvt, cos, sin = post_in
        _kv_rows(y, kg[...], wdkv, kgl[...], wuk, wuvt, cos[...], sin[...], *post_out)


def _layer(h, g, ffn_w, layer, pre="plain", pre_args=(), post="plain", post_args=(), with_meta=True):
    n_tiles = NT if with_meta else NT_REAL
    row_spec = pl.BlockSpec((TM, D), lambda i: (i, 0))
    vec = lambda v: v.reshape(1, -1)

    args, in_specs = [h], [row_spec]
    if pre == "first":
        in_specs[0] = pl.BlockSpec((TM, D), lambda i: (jnp.minimum(i, NT_REAL - 1), 0))
        args += list(pre_args)
        in_specs += [_const_spec((TM, D))]
    elif pre == "pool":
        def halo_map(i):
            first = (i % TILES_PER_SEQ) == 0
            return (jnp.where(first, META_ROW0 // HALO, i * (TM // HALO) - 1), 0)
        pg, pw, ps = pre_args
        args += [h, vec(pg), pw, vec(ps)]
        in_specs += [pl.BlockSpec((HALO, D), halo_map), _const_spec((1, D)),
                     _const_spec((len(POOL_WINDOWS), POOL_GROUP, POOL_GROUP)), _const_spec((1, D))]
    elif pre == "wo":
        args += list(pre_args)
        in_specs += [pl.BlockSpec((H * V_HEAD, TM), lambda i: (0, i)), _const_spec((H * V_HEAD, D))]

    args += [vec(g), *ffn_w]
    in_specs += [_const_spec((1, D)), _layer_spec((D, D_FF), layer), _layer_spec((D, D_FF), layer),
                 _layer_spec((D_FF, D), layer)]

    out_shape = [jax.ShapeDtypeStruct((n_tiles * TM, D), F32)]
    out_specs = [row_spec]
    if post == "final":
        args += [vec(post_args[0])]
        in_specs += [_const_spec((1, D))]
    elif post == "q":
        qg, wdq, qgl, wqt, wqrt, cos_t, sin_t = post_args
        tab_spec = pl.BlockSpec((QK_ROPE, TM), lambda i: (0, _tab_tile(i)))
        args += [vec(qg), wdq, vec(qgl), wqt, wqrt, cos_t, sin_t]
        in_specs += [_const_spec((1, D)), _const_spec((D, Q_RANK)), _const_spec((1, Q_RANK)),
                     _const_spec((H * LANES, Q_RANK)), _const_spec((H * QK_ROPE, Q_RANK)),
                     tab_spec, tab_spec]
        out_shape += [jax.ShapeDtypeStruct((H, LANES, n_tiles * TM), BF16)]
        out_specs += [pl.BlockSpec((H, LANES, TM), lambda i: (0, 0, i))]
    elif post == "kv":
        kg, wdkv, kgl, wuk, wuvt, cos, sin = post_args
        tab_spec = pl.BlockSpec((TM, LANES), lambda i: (_tab_tile(i), 0))
        args += [vec(kg), wdkv, vec(kgl), wuk, wuvt, cos, sin]
        in_specs += [_const_spec((1, D)), _const_spec((D, KV_RANK + 2 * LANES)),
                     _const_spec((1, KV_RANK)), _const_spec((KV_RANK, H * LANES)),
                     _const_spec((H * V_HEAD, KV_RANK)), tab_spec, tab_spec]
        out_shape += [jax.ShapeDtypeStruct((H, ROWS, LANES), BF16),
                      jax.ShapeDtypeStruct((H * VR, ROWS), BF16)]
        out_specs += [pl.BlockSpec((H, TM, LANES), lambda i: (0, i, 0)),
                      pl.BlockSpec((H * VR, TM), lambda i: (0, i))]

    outs = pl.pallas_call(
        functools.partial(_layer_kernel, pre=pre, post=post),
        grid=(n_tiles,),
        in_specs=in_specs,
        out_specs=out_specs,
        out_shape=out_shape,
        scratch_shapes=[pltpu.VMEM((TM, D), BF16), pltpu.VMEM((TM, D_FF), BF16)],
        compiler_params=_params(),
        name=f"layer_{pre}_{post}",
    )(*args)
    return outs[0] if len(outs) == 1 else outs


def _attn_kernel(qt_ref, k_ref, vt_ref, km_ref, vmt_ref, o_ref, m_ref, acc_ref, s_ref):
    j = pl.program_id(1)

    def head_rows(hh):
        return slice(hh * VR, (hh + 1) * VR)

    def keys_of(hh, u, kb0):
        if u == META:
            return km_ref[hh], vmt_ref[head_rows(hh), :]
        off = pl.multiple_of((kb0 + u) * KB, KB)
        return k_ref[hh, pl.ds(off, KB), :], vt_ref[head_rows(hh), pl.ds(off, KB)]

    def qk(hh, c, u, kb0):
        return jnp.dot(keys_of(hh, u, kb0)[0], qt_ref[hh, :, c * KB:(c + 1) * KB],
                       preferred_element_type=F32)

    kpos = lax.broadcasted_iota(jnp.int32, (KB, KB), 0)
    qpos = lax.broadcasted_iota(jnp.int32, (KB, KB), 1)
    chunk_mask = (kpos // CHUNK) <= (qpos // CHUNK)
    meta_mask = lax.broadcasted_iota(jnp.int32, (LANES, KB), 0) < N_META

    def run_tiles(tiles, kb0, tail):
        chains = sorted({(hh, c) for hh, c, _, _ in tiles})
        state = {}
        for hh, c in chains:
            cs = slice(c * KB, (c + 1) * KB)
            state[hh, c] = (m_ref[hh, :, cs], acc_ref[head_rows(hh), cs])
        scores = {}
        pending = [(t, (hh, c, u, kb0)) for t, (hh, c, u, _) in enumerate(tiles)][LOOKAHEAD:]
        pending += [(len(tiles) + i, a) for i, a in enumerate(tail)]

        def issue():
            if pending:
                t, a = pending.pop(0)
                if t < len(tiles):
                    scores[t] = qk(*a)
                else:
                    s_ref[t - len(tiles)] = qk(*a)

        for t, (hh, c, u, masked) in enumerate(tiles):
            m_old, acc = state[hh, c]
            s = s_ref[t] if t < LOOKAHEAD else scores.pop(t)
            if u == META:
                s = jnp.where(meta_mask, s, NEG)
            elif masked:
                s = jnp.where(chunk_mask, s, NEG)
            m_new = jnp.maximum(m_old, jnp.max(s, axis=0, keepdims=True))
            alpha = jnp.exp2(m_old - m_new)
            p = jnp.exp2((s - m_new).astype(BF16))
            issue()
            acc = alpha * acc + jnp.dot(keys_of(hh, u, kb0)[1], p, preferred_element_type=F32)
            state[hh, c] = (m_new, acc)
        for hh, c in chains:
            cs = slice(c * KB, (c + 1) * KB)
            m_ref[hh, :, cs], acc_ref[head_rows(hh), cs] = state[hh, c]

    full_tiles = [(hh, c, u, False) for u in range(NS) for hh in range(HP) for c in range(NS)]
    diag_tiles = [(hh, c, d, c == d) for d in range(NS) for hh in range(HP) for c in range(d, NS)]
    meta_tiles = [(hh, c, META, True) for hh in range(HP) for c in range(NS)]
    assert [t[:3] for t in full_tiles[:LOOKAHEAD]] == [t[:3] for t in diag_tiles[:LOOKAHEAD]]

    def first_scores(kb0):
        return [(hh, c, u, kb0) for hh, c, u, _ in full_tiles[:LOOKAHEAD]]

    q_tile = j % TILES_PER_SEQ

    m_ref[...] = jnp.full(m_ref.shape, NEG, F32)
    acc_ref[...] = jnp.zeros(acc_ref.shape, F32)
    for t, a in enumerate(first_scores(0)):
        s_ref[t] = qk(*a)

    pair_tiles = [(hh, c, u, False) for u in range(2 * NS) for hh in range(HP) for c in range(NS)]

    def pair_body(it, carry):
        run_tiles(pair_tiles, it * 2 * NS, first_scores((it + 1) * 2 * NS))
        return carry

    lax.fori_loop(0, q_tile // 2, pair_body, 0)

    @pl.when(q_tile % 2 == 1)
    def _():
        run_tiles(full_tiles, (q_tile - 1) * NS, first_scores(q_tile * NS))

    run_tiles(diag_tiles + meta_tiles, q_tile * NS, [])

    for hh in range(HP):
        acc = acc_ref[head_rows(hh), :]
        o_ref[hh * V_HEAD:(hh + 1) * V_HEAD, :] = (
            acc[:V_HEAD, :] / acc[V_HEAD:V_HEAD + 1, :]).astype(BF16)


def _attn(qt_all, k_all, vt_all):
    def b_of(j):
        return j // TILES_PER_SEQ

    return pl.pallas_call(
        _attn_kernel,
        grid=(H // HP, NT_REAL),
        in_specs=[pl.BlockSpec((HP, LANES, BQ), lambda h, j: (h, 0, j)),
                  pl.BlockSpec((HP, SEQ, LANES), lambda h, j: (h, b_of(j), 0),
                               pipeline_mode=pl.Buffered(1)),
                  pl.BlockSpec((HP * VR, SEQ), lambda h, j: (h, b_of(j)),
                               pipeline_mode=pl.Buffered(1)),
                  pl.BlockSpec((HP, LANES, LANES), lambda h, j: (h, META_ROW0 // LANES, 0)),
                  pl.BlockSpec((HP * VR, LANES), lambda h, j: (h, META_ROW0 // LANES))],
        out_specs=pl.BlockSpec((HP * V_HEAD, BQ), lambda h, j: (h, j)),
        out_shape=jax.ShapeDtypeStruct((H * V_HEAD, B * SEQ), BF16),
        scratch_shapes=[pltpu.VMEM((HP, 1, BQ), F32), pltpu.VMEM((HP * VR, BQ), F32),
                        pltpu.VMEM((LOOKAHEAD, KB, KB), F32)],
        compiler_params=_params(2),
        name="attn",
    )(qt_all, k_all, vt_all, k_all, vt_all)


def _rot_cols(w):
    half = QK_ROPE // 2
    return jnp.concatenate([-w[..., half:], w[..., :half]], axis=-1)


def _place_rope(w):
    pad = [(0, 0)] * (w.ndim - 1) + [(QK_NOPE, LANES - QK_NOPE - QK_ROPE)]
    return jnp.pad(w, pad)


def _rope_tables():
    inv = 1.0 / (ROPE_THETA ** (jnp.arange(0, QK_ROPE, 2, dtype=F32) / QK_ROPE))
    row = jnp.arange(TAB_ROWS)
    pos = jnp.where(row < SEQ, N_META + row, row - SEQ).astype(F32)
    ang = pos[:, None] * inv[None, :]
    ones = jnp.ones((TAB_ROWS, QK_NOPE), F32)
    ones_hi = jnp.ones((TAB_ROWS, LANES - QK_NOPE - QK_ROPE), F32)
    cos_t = jnp.concatenate([ones, jnp.cos(ang), jnp.cos(ang), ones_hi], axis=1)
    sin_t = jnp.concatenate([0 * ones, jnp.sin(ang), jnp.sin(ang), 0 * ones_hi], axis=1)
    return cos_t, sin_t


def kernel(x, meta_tokens, ffn1_norm, ffn1_w_gate, ffn1_w_up, ffn1_w_down, mix_norm, ffn2_norm, ffn2_w_gate, ffn2_w_up, ffn2_w_down, pool_w, pool_scale, kv_in_norm, w_dkv, kv_latent_norm, w_uk, w_uv, w_dq, q_latent_norm, w_uq, w_o, final_norm):
    cos_t, sin_t = _rope_tables()
    frames = x.reshape(B * SEQ, D)
    meta_tile = jnp.concatenate(
        [meta_tokens.astype(x.dtype), jnp.zeros((TM - N_META, D), x.dtype)], axis=0)
    ffn1_w = [ffn1_w_gate, ffn1_w_up, ffn1_w_down]
    ffn2_w = [ffn2_w_gate, ffn2_w_up, ffn2_w_down]

    w_kr = w_dkv[:, KV_RANK:]
    wd_kv = jnp.concatenate(
        [w_dkv[:, :KV_RANK], _place_rope(w_kr), _place_rope(_rot_cols(w_kr))], axis=1).astype(BF16)
    wuk = jnp.pad(w_uk.reshape(KV_RANK, H, QK_NOPE),
                  ((0, 0), (0, 0), (0, LANES - QK_NOPE))).reshape(KV_RANK, H * LANES).astype(BF16)
    kv_args = (kv_in_norm, wd_kv, kv_latent_norm, wuk, w_uv.T.astype(BF16), cos_t, sin_t)

    def q_args(l):
        jj = l - N_A
        wq3 = w_uq[jj].reshape(Q_RANK, H, QK_NOPE + QK_ROPE)
        wq = jnp.pad(wq3, ((0, 0), (0, 0), (0, LANES - QK_NOPE - QK_ROPE)))
        wqr = _rot_cols(wq3[:, :, QK_NOPE:])
        to_t = lambda w: w.reshape(Q_RANK, -1).T.astype(BF16)
        rope_rows = slice(QK_NOPE, QK_NOPE + QK_ROPE)
        return (mix_norm[l], w_dq[jj].astype(BF16), q_latent_norm[jj], to_t(wq), to_t(wqr),
                cos_t[:, rope_rows].T, sin_t[:, rope_rows].T)

    h = _layer(frames, ffn1_norm[0], ffn1_w, 0, pre="first", pre_args=(meta_tile,))
    h = _layer(h, ffn2_norm[0], ffn2_w, 0, pre="pool",
               pre_args=(mix_norm[0], pool_w[0].astype(BF16), pool_scale[0]))
    h = _layer(h, ffn1_norm[1], ffn1_w, 1)
    h, k_all, vt_all = _layer(h, ffn2_norm[1], ffn2_w, 1, pre="pool",
                              pre_args=(mix_norm[1], pool_w[1].astype(BF16), pool_scale[1]),
                              post="kv", post_args=kv_args)
    for l in range(N_A, DEPTH):
        h, qt_all = _layer(h, ffn1_norm[l], ffn1_w, l, post="q", post_args=q_args(l),
                           with_meta=False)
        ot = _attn(qt_all, k_all, vt_all)
        last = l == DEPTH - 1
        h = _layer(h, ffn2_norm[l], ffn2_w, l, pre="wo", pre_args=(ot, w_o[l - N_A].astype(BF16)),
                   post="final" if last else "plain", post_args=(final_norm,) if last else (),
                   with_meta=False)
    return h.reshape(B, SEQ, D)
```

```python
import functools
import math

import jax
import jax.numpy as jnp
from jax import lax
from jax.experimental import pallas as pl
from jax.experimental.pallas import tpu as pltpu

D = 1024
B = 2
SEQ = 8192
DEPTH = 4
CHUNK = 64
N_META = 16
N_A = DEPTH // 2
D_FF = 2816
POOL_WINDOWS = (2, 4, 8, 16)
POOL_GROUP = D // len(POOL_WINDOWS)
H = 8
QK_NOPE = 64
QK_ROPE = 32
V_HEAD = 64
KV_RANK = 256
Q_RANK = 384
ROPE_THETA = 10000.0
EPS = 1e-6

F32 = jnp.float32
BF16 = jnp.bfloat16

LANES = 128
TM = 512
NT_REAL = B * SEQ // TM
NT = NT_REAL + 1
ROWS = NT * TM
TILES_PER_SEQ = SEQ // TM
META_ROW0 = B * SEQ
TAB_ROWS = SEQ + TM
FC = 256
KB = 256
BQ = TM
NS = BQ // KB
HP = H
RING = 8
LOOKAHEAD = 6
HALO = 16
NEG = -1e30
META = -1
VR = V_HEAD + 16
VMEM_LIMIT = 56 * 1024 * 1024
NT_DIMS = (((1,), (1,)), ((), ()))


def _rms(x, g):
    ms = jnp.mean(x * x, axis=-1, keepdims=True)
    return (x * lax.rsqrt(ms + EPS)) * g


def _params(n_axes=1):
    return pltpu.CompilerParams(
        dimension_semantics=("arbitrary",) * n_axes, vmem_limit_bytes=VMEM_LIMIT)


def _const_spec(shape):
    zeros = (0,) * len(shape)
    return pl.BlockSpec(shape, lambda i: zeros, pipeline_mode=pl.Buffered(1))


def _layer_spec(shape, layer):
    return pl.BlockSpec((None,) + shape, lambda i: (layer, 0, 0), pipeline_mode=pl.Buffered(1))


def _tab_tile(i):
    return jnp.where(i < NT_REAL, i % TILES_PER_SEQ, TILES_PER_SEQ)


def _pool_rows(x, halo, is_meta, g, w_ref, scale):
    u = _rms(x, g)
    uh = _rms(halo, g) * jnp.where(is_meta, 0.0, 1.0)
    n_avail = lax.broadcasted_iota(jnp.int32, (HALO, POOL_GROUP), 0) + jnp.where(is_meta, 1, HALO)
    pad = jnp.zeros((HALO, POOL_GROUP), F32)
    outs = []
    for gi, w in enumerate(POOL_WINDOWS):
        sl = slice(gi * POOL_GROUP, (gi + 1) * POOL_GROUP)
        ug = u[:, sl]
        acc = jnp.concatenate([pad, uh[:, sl], ug], axis=0)
        step = 1
        while step < w:
            acc = acc + pltpu.roll(acc, step, 0)
            step *= 2
        head = acc[2 * HALO:3 * HALO, :] / jnp.minimum(n_avail, w).astype(F32)
        pooled = jnp.concatenate([head, acc[3 * HALO:, :] * (1.0 / w)], axis=0) - ug
        y = jnp.dot(pooled.astype(BF16), w_ref[gi], preferred_element_type=F32)
        outs.append(x[:, sl] + y * scale[:, sl])
    return jnp.concatenate(outs, axis=1)


def _q_rows(y, g, wdq_ref, gl, wqt_ref, wqrt_ref, cos_t, sin_t, q_ref):
    u = _rms(y, g).astype(BF16)
    cq = jnp.dot(u, wdq_ref[...], preferred_element_type=F32)
    cqn = _rms(cq, gl).astype(BF16)
    qa = lax.dot_general(wqt_ref[...], cqn, NT_DIMS, preferred_element_type=F32)
    qb = lax.dot_general(wqrt_ref[...], cqn, NT_DIMS, preferred_element_type=F32)
    scale = (1.0 / math.sqrt(QK_NOPE + QK_ROPE)) * math.log2(math.e)
    lo, hi = QK_NOPE, QK_NOPE + QK_ROPE
    for hh in range(H):
        a = qa[hh * LANES:(hh + 1) * LANES, :]
        rope = a[lo:hi, :] * cos_t + qb[hh * QK_ROPE:(hh + 1) * QK_ROPE, :] * sin_t
        q_ref[hh] = (jnp.concatenate([a[:lo, :], rope, a[hi:, :]], axis=0) * scale).astype(BF16)


def _kv_rows(y, g, wd_ref, gl, wuk_ref, wuvt_ref, cos, sin, k_ref, vt_ref):
    xn = _rms(y, g).astype(BF16)
    ckr = jnp.dot(xn, wd_ref[...], preferred_element_type=F32)
    ckv = _rms(ckr[:, :KV_RANK], gl).astype(BF16)
    kr = ckr[:, KV_RANK:KV_RANK + LANES] * cos + ckr[:, KV_RANK + LANES:] * sin
    kn = jnp.dot(ckv, wuk_ref[...], preferred_element_type=F32)
    for hh in range(H):
        k_ref[hh] = (kn[:, hh * LANES:(hh + 1) * LANES] + kr).astype(BF16)
    vt = lax.dot_general(wuvt_ref[...], ckv, NT_DIMS, preferred_element_type=F32)
    ones_row = (lax.broadcasted_iota(jnp.int32, (VR - V_HEAD, TM), 0) == 0).astype(BF16)
    for hh in range(H):
        vt_ref[hh * VR:hh * VR + V_HEAD, :] = vt[hh * V_HEAD:(hh + 1) * V_HEAD, :].astype(BF16)
        vt_ref[hh * VR + V_HEAD:(hh + 1) * VR, :] = ones_row


def _layer_kernel(*refs, pre, post):
    refs = list(refs)

    def take(n):
        return [refs.pop(0) for _ in range(n)]

    i = pl.program_id(0)
    is_meta = i == NT - 1
    (x_ref,) = take(1)
    x = x_ref[...]
    if pre == "first":
        (xm_ref,) = take(1)
        x = jnp.where(is_meta, xm_ref[...], x)
    elif pre == "pool":
        halo_ref, pg_ref, pw_ref, ps_ref = take(4)
        x = _pool_rows(x, halo_ref[...], is_meta, pg_ref[...], pw_ref, ps_ref[...])
    elif pre == "wo":
        ot_ref, wo_ref = take(2)
        x = x + lax.dot_general(ot_ref[...], wo_ref[...], (((0,), (0,)), ((), ())),
                                preferred_element_type=F32)
    g_ref, wg_ref, wu_ref, wd_ref = take(4)
    post_in = take({"plain": 0, "final": 1, "q": 7, "kv": 7}[post])
    (o_ref,) = take(1)
    post_out = take({"plain": 0, "final": 0, "q": 1, "kv": 2}[post])
    xn_ref, a_ref = take(2)
    assert not refs

    xn_ref[...] = _rms(x, g_ref[...]).astype(BF16)
    for c in range(D_FF // FC):
        sl = slice(c * FC, (c + 1) * FC)
        gate = jnp.dot(xn_ref[...], wg_ref[:, sl].astype(BF16), preferred_element_type=F32)
        up = jnp.dot(xn_ref[...], wu_ref[:, sl].astype(BF16), preferred_element_type=F32)
        a_ref[:, sl] = (gate * jax.nn.sigmoid(gate) * up).astype(BF16)
    y = jnp.concatenate(
        [x[:, n * FC:(n + 1) * FC] + 0.5 * jnp.dot(
            a_ref[...], wd_ref[:, n * FC:(n + 1) * FC].astype(BF16), preferred_element_type=F32)
         for n in range(D // FC)], axis=1)

    if post == "final":
        y = _rms(y, post_in[0][...])
    o_ref[...] = y
    if post == "q":
        qg, wdq, qgl, wqt, wqrt, cos_t, sin_t = post_in
        _q_rows(y, qg[...], wdq, qgl[...], wqt, wqrt, cos_t[...], sin_t[...], post_out[0])
    elif post == "kv":
        kg, wdkv, kgl, wuk, wuvt, cos, sin = post_in
        _kv_rows(y, kg[...], wdkv, kgl[...], wuk, wuvt, cos[...], sin[...], *post_out)


def _layer(h, g, ffn_w, layer, pre="plain", pre_args=(), post="plain", post_args=(), with_meta=True):
    n_tiles = NT if with_meta else NT_REAL
    row_spec = pl.BlockSpec((TM, D), lambda i: (i, 0))
    vec = lambda v: v.reshape(1, -1)

    args, in_specs = [h], [row_spec]
    if pre == "first":
        in_specs[0] = pl.BlockSpec((TM, D), lambda i: (jnp.minimum(i, NT_REAL - 1), 0))
        args += list(pre_args)
        in_specs += [_const_spec((TM, D))]
    elif pre == "pool":
        def halo_map(i):
            first = (i % TILES_PER_SEQ) == 0
            return (jnp.where(first, META_ROW0 // HALO, i * (TM // HALO) - 1), 0)
        pg, pw, ps = pre_args
        args += [h, vec(pg), pw, vec(ps)]
        in_specs += [pl.BlockSpec((HALO, D), halo_map), _const_spec((1, D)),
                     _const_spec((len(POOL_WINDOWS), POOL_GROUP, POOL_GROUP)), _const_spec((1, D))]
    elif pre == "wo":
        args += list(pre_args)
        in_specs += [pl.BlockSpec((H * V_HEAD, TM), lambda i: (0, i)), _const_spec((H * V_HEAD, D))]

    args += [vec(g), *ffn_w]
    in_specs += [_const_spec((1, D)), _layer_spec((D, D_FF), layer), _layer_spec((D, D_FF), layer),
                 _layer_spec((D_FF, D), layer)]

    out_shape = [jax.ShapeDtypeStruct((n_tiles * TM, D), F32)]
    out_specs = [row_spec]
    if post == "final":
        args += [vec(post_args[0])]
        in_specs += [_const_spec((1, D))]
    elif post == "q":
        qg, wdq, qgl, wqt, wqrt, cos_t, sin_t = post_args
        tab_spec = pl.BlockSpec((QK_ROPE, TM), lambda i: (0, _tab_tile(i)))
        args += [vec(qg), wdq, vec(qgl), wqt, wqrt, cos_t, sin_t]
        in_specs += [_const_spec((1, D)), _const_spec((D, Q_RANK)), _const_spec((1, Q_RANK)),
                     _const_spec((H * LANES, Q_RANK)), _const_spec((H * QK_ROPE, Q_RANK)),
                     tab_spec, tab_spec]
        out_shape += [jax.ShapeDtypeStruct((H, LANES, n_tiles * TM), BF16)]
        out_specs += [pl.BlockSpec((H, LANES, TM), lambda i: (0, 0, i))]
    elif post == "kv":
        kg, wdkv, kgl, wuk, wuvt, cos, sin = post_args
        tab_spec = pl.BlockSpec((TM, LANES), lambda i: (_tab_tile(i), 0))
        args += [vec(kg), wdkv, vec(kgl), wuk, wuvt, cos, sin]
        in_specs += [_const_spec((1, D)), _const_spec((D, KV_RANK + 2 * LANES)),
                     _const_spec((1, KV_RANK)), _const_spec((KV_RANK, H * LANES)),
                     _const_spec((H * V_HEAD, KV_RANK)), tab_spec, tab_spec]
        out_shape += [jax.ShapeDtypeStruct((H, ROWS, LANES), BF16),
                      jax.ShapeDtypeStruct((H * VR, ROWS), BF16)]
        out_specs += [pl.BlockSpec((H, TM, LANES), lambda i: (0, i, 0)),
                      pl.BlockSpec((H * VR, TM), lambda i: (0, i))]

    outs = pl.pallas_call(
        functools.partial(_layer_kernel, pre=pre, post=post),
        grid=(n_tiles,),
        in_specs=in_specs,
        out_specs=out_specs,
        out_shape=out_shape,
        scratch_shapes=[pltpu.VMEM((TM, D), BF16), pltpu.VMEM((TM, D_FF), BF16)],
        compiler_params=_params(),
        name=f"layer_{pre}_{post}",
    )(*args)
    return outs[0] if len(outs) == 1 else outs


def _attn_kernel(qt_ref, k_ref, vt_ref, km_ref, vmt_ref, o_ref, m_ref, acc_ref, s_ref):
    j = pl.program_id(1)

    def head_rows(hh):
        return slice(hh * VR, (hh + 1) * VR)

    def keys_of(hh, u, kb0):
        if u == META:
            return km_ref[hh], vmt_ref[head_rows(hh), :]
        off = pl.multiple_of((kb0 + u) * KB, KB)
        return k_ref[hh, pl.ds(off, KB), :], vt_ref[head_rows(hh), pl.ds(off, KB)]

    def qk(hh, c, u, kb0):
        return jnp.dot(keys_of(hh, u, kb0)[0], qt_ref[hh, :, c * KB:(c + 1) * KB],
                       preferred_element_type=F32)

    kpos = lax.broadcasted_iota(jnp.int32, (KB, KB), 0)
    qpos = lax.broadcasted_iota(jnp.int32, (KB, KB), 1)
    chunk_mask = (kpos // CHUNK) <= (qpos // CHUNK)
    meta_mask = lax.broadcasted_iota(jnp.int32, (LANES, KB), 0) < N_META

    def run_tiles(tiles, kb0, tail):
        chains = sorted({(hh, c) for hh, c, _, _ in tiles})
        state = {}
        for hh, c in chains:
            cs = slice(c * KB, (c + 1) * KB)
            state[hh, c] = (m_ref[hh, :, cs], acc_ref[head_rows(hh), cs])
        assert len(tiles) % RING == 0
        pending = [(t, (hh, c, u, kb0)) for t, (hh, c, u, _) in enumerate(tiles)][LOOKAHEAD:]
        pending += [(len(tiles) + i, a) for i, a in enumerate(tail)]

        def issue():
            if pending:
                t, a = pending.pop(0)
                rows = LANES if a[2] == META else KB
                s_ref[t % RING, :rows, :] = qk(*a)

        def read(t, u, masked):
            if u == META:
                return jnp.where(meta_mask, s_ref[t % RING, :LANES, :], NEG)
            s = s_ref[t % RING]
            return jnp.where(chunk_mask, s, NEG) if masked else s

        for t, (hh, c, u, masked) in enumerate(tiles):
            m_old, acc = state[hh, c]
            m_new = jnp.maximum(m_old, jnp.max(read(t, u, masked), axis=0, keepdims=True))
            alpha = jnp.exp2(m_old - m_new)
            p = jnp.exp2((read(t, u, masked) - m_new).astype(BF16))
            issue()
            acc = alpha * acc + jnp.dot(keys_of(hh, u, kb0)[1], p, preferred_element_type=F32)
            state[hh, c] = (m_new, acc)
        for hh, c in chains:
            cs = slice(c * KB, (c + 1) * KB)
            m_ref[hh, :, cs], acc_ref[head_rows(hh), cs] = state[hh, c]

    full_tiles = [(hh, c, u, False) for u in range(NS) for hh in range(HP) for c in range(NS)]
    diag_tiles = [(hh, c, d, c == d) for d in range(NS) for hh in range(HP) for c in range(d, NS)]
    meta_tiles = [(hh, c, META, True) for hh in range(HP) for c in range(NS)]
    assert [t[:3] for t in full_tiles[:LOOKAHEAD]] == [t[:3] for t in diag_tiles[:LOOKAHEAD]]

    def first_scores(kb0):
        return [(hh, c, u, kb0) for hh, c, u, _ in full_tiles[:LOOKAHEAD]]

    q_tile = j % TILES_PER_SEQ

    m_ref[...] = jnp.full(m_ref.shape, NEG, F32)
    acc_ref[...] = jnp.zeros(acc_ref.shape, F32)
    for t, a in enumerate(first_scores(0)):
        s_ref[t] = qk(*a)

    pair_tiles = [(hh, c, u, False) for u in range(2 * NS) for hh in range(HP) for c in range(NS)]

    def pair_body(it, carry):
        run_tiles(pair_tiles, it * 2 * NS, first_scores((it + 1) * 2 * NS))
        return carry

    lax.fori_loop(0, q_tile // 2, pair_body, 0)

    @pl.when(q_tile % 2 == 1)
    def _():
        run_tiles(full_tiles, (q_tile - 1) * NS, first_scores(q_tile * NS))

    run_tiles(diag_tiles + meta_tiles, q_tile * NS, [])

    for hh in range(HP):
        acc = acc_ref[head_rows(hh), :]
        o_ref[hh * V_HEAD:(hh + 1) * V_HEAD, :] = (
            acc[:V_HEAD, :] / acc[V_HEAD:V_HEAD + 1, :]).astype(BF16)


def _attn(qt_all, k_all, vt_all):
    def b_of(j):
        return j // TILES_PER_SEQ

    return pl.pallas_call(
        _attn_kernel,
        grid=(H // HP, NT_REAL),
        in_specs=[pl.BlockSpec((HP, LANES, BQ), lambda h, j: (h, 0, j)),
                  pl.BlockSpec((HP, SEQ, LANES), lambda h, j: (h, b_of(j), 0),
                               pipeline_mode=pl.Buffered(1)),
                  pl.BlockSpec((HP * VR, SEQ), lambda h, j: (h, b_of(j)),
                               pipeline_mode=pl.Buffered(1)),
                  pl.BlockSpec((HP, LANES, LANES), lambda h, j: (h, META_ROW0 // LANES, 0)),
                  pl.BlockSpec((HP * VR, LANES), lambda h, j: (h, META_ROW0 // LANES))],
        out_specs=pl.BlockSpec((HP * V_HEAD, BQ), lambda h, j: (h, j)),
        out_shape=jax.ShapeDtypeStruct((H * V_HEAD, B * SEQ), BF16),
        scratch_shapes=[pltpu.VMEM((HP, 1, BQ), F32), pltpu.VMEM((HP * VR, BQ), F32),
                        pltpu.VMEM((RING, KB, KB), F32)],
        compiler_params=_params(2),
        name="attn",
    )(qt_all, k_all, vt_all, k_all, vt_all)


def _rot_cols(w):
    half = QK_ROPE // 2
    return jnp.concatenate([-w[..., half:], w[..., :half]], axis=-1)


def _place_rope(w):
    pad = [(0, 0)] * (w.ndim - 1) + [(QK_NOPE, LANES - QK_NOPE - QK_ROPE)]
    return jnp.pad(w, pad)


def _rope_tables():
    inv = 1.0 / (ROPE_THETA ** (jnp.arange(0, QK_ROPE, 2, dtype=F32) / QK_ROPE))
    row = jnp.arange(TAB_ROWS)
    pos = jnp.where(row < SEQ, N_META + row, row - SEQ).astype(F32)
    ang = pos[:, None] * inv[None, :]
    ones = jnp.ones((TAB_ROWS, QK_NOPE), F32)
    ones_hi = jnp.ones((TAB_ROWS, LANES - QK_NOPE - QK_ROPE), F32)
    cos_t = jnp.concatenate([ones, jnp.cos(ang), jnp.cos(ang), ones_hi], axis=1)
    sin_t = jnp.concatenate([0 * ones, jnp.sin(ang), jnp.sin(ang), 0 * ones_hi], axis=1)
    return cos_t, sin_t


def kernel(x, meta_tokens, ffn1_norm, ffn1_w_gate, ffn1_w_up, ffn1_w_down, mix_norm, ffn2_norm, ffn2_w_gate, ffn2_w_up, ffn2_w_down, pool_w, pool_scale, kv_in_norm, w_dkv, kv_latent_norm, w_uk, w_uv, w_dq, q_latent_norm, w_uq, w_o, final_norm):
    cos_t, sin_t = _rope_tables()
    frames = x.reshape(B * SEQ, D)
    meta_tile = jnp.concatenate(
        [meta_tokens.astype(x.dtype), jnp.zeros((TM - N_META, D), x.dtype)], axis=0)
    ffn1_w = [ffn1_w_gate, ffn1_w_up, ffn1_w_down]
    ffn2_w = [ffn2_w_gate, ffn2_w_up, ffn2_w_down]

    w_kr = w_dkv[:, KV_RANK:]
    wd_kv = jnp.concatenate(
        [w_dkv[:, :KV_RANK], _place_rope(w_kr), _place_rope(_rot_cols(w_kr))], axis=1).astype(BF16)
    wuk = jnp.pad(w_uk.reshape(KV_RANK, H, QK_NOPE),
                  ((0, 0), (0, 0), (0, LANES - QK_NOPE))).reshape(KV_RANK, H * LANES).astype(BF16)
    kv_args = (kv_in_norm, wd_kv, kv_latent_norm, wuk, w_uv.T.astype(BF16), cos_t, sin_t)

    def q_args(l):
        jj = l - N_A
        wq3 = w_uq[jj].reshape(Q_RANK, H, QK_NOPE + QK_ROPE)
        wq = jnp.pad(wq3, ((0, 0), (0, 0), (0, LANES - QK_NOPE - QK_ROPE)))
        wqr = _rot_cols(wq3[:, :, QK_NOPE:])
        to_t = lambda w: w.reshape(Q_RANK, -1).T.astype(BF16)
        rope_rows = slice(QK_NOPE, QK_NOPE + QK_ROPE)
        return (mix_norm[l], w_dq[jj].astype(BF16), q_latent_norm[jj], to_t(wq), to_t(wqr),
                cos_t[:, rope_rows].T, sin_t[:, rope_rows].T)

    h = _layer(frames, ffn1_norm[0], ffn1_w, 0, pre="first", pre_args=(meta_tile,))
    h = _layer(h, ffn2_norm[0], ffn2_w, 0, pre="pool",
               pre_args=(mix_norm[0], pool_w[0].astype(BF16), pool_scale[0]))
    h = _layer(h, ffn1_norm[1], ffn1_w, 1)
    h, k_all, vt_all = _layer(h, ffn2_norm[1], ffn2_w, 1, pre="pool",
                              pre_args=(mix_norm[1], pool_w[1].astype(BF16), pool_scale[1]),
                              post="kv", post_args=kv_args)
    for l in range(N_A, DEPTH):
        h, qt_all = _layer(h, ffn1_norm[l], ffn1_w, l, post="q", post_args=q_args(l),
                           with_meta=False)
        ot = _attn(qt_all, k_all, vt_all)
        last = l == DEPTH - 1
        h = _layer(h, ffn2_norm[l], ffn2_w, l, pre="wo", pre_args=(ot, w_o[l - N_A].astype(BF16)),
                   post="final" if last else "plain", post_args=(final_norm,) if last else (),
                   with_meta=False)
    return h.reshape(B, SEQ, D)
```

```python
import functools
import math

import jax
import jax.numpy as jnp
from jax import lax
from jax.experimental import pallas as pl
from jax.experimental.pallas import tpu as pltpu

D = 1024
B = 2
SEQ = 8192
DEPTH = 4
CHUNK = 64
N_META = 16
N_A = DEPTH // 2
D_FF = 2816
POOL_WINDOWS = (2, 4, 8, 16)
POOL_GROUP = D // len(POOL_WINDOWS)
H = 8
QK_NOPE = 64
QK_ROPE = 32
V_HEAD = 64
KV_RANK = 256
Q_RANK = 384
ROPE_THETA = 10000.0
EPS = 1e-6

F32 = jnp.float32
BF16 = jnp.bfloat16

LANES = 128
TM = 512
NT_REAL = B * SEQ // TM
NT = NT_REAL + 1
ROWS = NT * TM
TILES_PER_SEQ = SEQ // TM
META_ROW0 = B * SEQ
TAB_ROWS = SEQ + TM
FC = 256
KB = 256
BQ = TM
NS = BQ // KB
HP = H
RING = 8
LOOKAHEAD = 6
HALO = 16
NEG = -1e30
META = -1
VR = V_HEAD + 16
VMEM_LIMIT = 56 * 1024 * 1024
NT_DIMS = (((1,), (1,)), ((), ()))


def _rms(x, g):
    ms = jnp.mean(x * x, axis=-1, keepdims=True)
    return (x * lax.rsqrt(ms + EPS)) * g


def _params(n_axes=1):
    return pltpu.CompilerParams(
        dimension_semantics=("arbitrary",) * n_axes, vmem_limit_bytes=VMEM_LIMIT)


def _const_spec(shape):
    zeros = (0,) * len(shape)
    return pl.BlockSpec(shape, lambda i: zeros, pipeline_mode=pl.Buffered(1))


def _layer_spec(shape, layer):
    return pl.BlockSpec((None,) + shape, lambda i: (layer, 0, 0), pipeline_mode=pl.Buffered(1))


def _tab_tile(i):
    return jnp.where(i < NT_REAL, i % TILES_PER_SEQ, TILES_PER_SEQ)


def _pool_rows(x, halo, is_meta, g, w_ref, scale):
    u = _rms(x, g)
    uh = _rms(halo, g) * jnp.where(is_meta, 0.0, 1.0)
    n_avail = lax.broadcasted_iota(jnp.int32, (HALO, POOL_GROUP), 0) + jnp.where(is_meta, 1, HALO)
    pad = jnp.zeros((HALO, POOL_GROUP), F32)
    outs = []
    for gi, w in enumerate(POOL_WINDOWS):
        sl = slice(gi * POOL_GROUP, (gi + 1) * POOL_GROUP)
        ug = u[:, sl]
        acc = jnp.concatenate([pad, uh[:, sl], ug], axis=0)
        step = 1
        while step < w:
            acc = acc + pltpu.roll(acc, step, 0)
            step *= 2
        head = acc[2 * HALO:3 * HALO, :] / jnp.minimum(n_avail, w).astype(F32)
        pooled = jnp.concatenate([head, acc[3 * HALO:, :] * (1.0 / w)], axis=0) - ug
        y = jnp.dot(pooled.astype(BF16), w_ref[gi], preferred_element_type=F32)
        outs.append(x[:, sl] + y * scale[:, sl])
    return jnp.concatenate(outs, axis=1)


def _q_rows(y, g, wdq_ref, gl, wqt_ref, wqrt_ref, cos_t, sin_t, q_ref):
    u = _rms(y, g).astype(BF16)
    cq = jnp.dot(u, wdq_ref[...], preferred_element_type=F32)
    cqn = _rms(cq, gl).astype(BF16)
    qa = lax.dot_general(wqt_ref[...], cqn, NT_DIMS, preferred_element_type=F32)
    qb = lax.dot_general(wqrt_ref[...], cqn, NT_DIMS, preferred_element_type=F32)
    scale = (1.0 / math.sqrt(QK_NOPE + QK_ROPE)) * math.log2(math.e)
    lo, hi = QK_NOPE, QK_NOPE + QK_ROPE
    for hh in range(H):
        a = qa[hh * LANES:(hh + 1) * LANES, :]
        rope = a[lo:hi, :] * cos_t + qb[hh * QK_ROPE:(hh + 1) * QK_ROPE, :] * sin_t
        q_ref[hh] = (jnp.concatenate([a[:lo, :], rope, a[hi:, :]], axis=0) * scale).astype(BF16)


def _kv_rows(y, g, wd_ref, gl, wuk_ref, wuvt_ref, cos, sin, k_ref, vt_ref):
    xn = _rms(y, g).astype(BF16)
    ckr = jnp.dot(xn, wd_ref[...], preferred_element_type=F32)
    ckv = _rms(ckr[:, :KV_RANK], gl).astype(BF16)
    kr = ckr[:, KV_RANK:KV_RANK + LANES] * cos + ckr[:, KV_RANK + LANES:] * sin
    kn = jnp.dot(ckv, wuk_ref[...], preferred_element_type=F32)
    for hh in range(H):
        k_ref[hh] = (kn[:, hh * LANES:(hh + 1) * LANES] + kr).astype(BF16)
    vt = lax.dot_general(wuvt_ref[...], ckv, NT_DIMS, preferred_element_type=F32)
    ones_row = (lax.broadcasted_iota(jnp.int32, (VR - V_HEAD, TM), 0) == 0).astype(BF16)
    for hh in range(H):
        vt_ref[hh * VR:hh * VR + V_HEAD, :] = vt[hh * V_HEAD:(hh + 1) * V_HEAD, :].astype(BF16)
        vt_ref[hh * VR + V_HEAD:(hh + 1) * VR, :] = ones_row


def _layer_kernel(*refs, pre, post):
    refs = list(refs)

    def take(n):
        return [refs.pop(0) for _ in range(n)]

    i = pl.program_id(0)
    is_meta = i == NT - 1
    (x_ref,) = take(1)
    x = x_ref[...]
    if pre == "first":
        (xm_ref,) = take(1)
        x = jnp.where(is_meta, xm_ref[...], x)
    elif pre == "pool":
        halo_ref, pg_ref, pw_ref, ps_ref = take(4)
        x = _pool_rows(x, halo_ref[...], is_meta, pg_ref[...], pw_ref, ps_ref[...])
    elif pre == "wo":
        ot_ref, wo_ref = take(2)
        x = x + lax.dot_general(ot_ref[...], wo_ref[...], (((0,), (0,)), ((), ())),
                                preferred_element_type=F32)
    g_ref, wg_ref, wu_ref, wd_ref = take(4)
    post_in = take({"plain": 0, "final": 1, "q": 7, "kv": 7}[post])
    (o_ref,) = take(1)
    post_out = take({"plain": 0, "final": 0, "q": 1, "kv": 2}[post])
    xn_ref, a_ref = take(2)
    assert not refs

    xn_ref[...] = _rms(x, g_ref[...]).astype(BF16)
    for c in range(D_FF // FC):
        sl = slice(c * FC, (c + 1) * FC)
        gate = jnp.dot(xn_ref[...], wg_ref[:, sl].astype(BF16), preferred_element_type=F32)
        up = jnp.dot(xn_ref[...], wu_ref[:, sl].astype(BF16), preferred_element_type=F32)
        a_ref[:, sl] = (gate * jax.nn.sigmoid(gate) * up).astype(BF16)
    y = jnp.concatenate(
        [x[:, n * FC:(n + 1) * FC] + 0.5 * jnp.dot(
            a_ref[...], wd_ref[:, n * FC:(n + 1) * FC].astype(BF16), preferred_element_type=F32)
         for n in range(D // FC)], axis=1)

    if post == "final":
        y = _rms(y, post_in[0][...])
    o_ref[...] = y
    if post == "q":
        qg, wdq, qgl, wqt, wqrt, cos_t, sin_t = post_in
        _q_rows(y, qg[...], wdq, qgl[...], wqt, wqrt, cos_t[...], sin_t[...], post_out[0])
    elif post == "kv":
        kg, wdkv, kgl, wuk, wuvt, cos, sin = post_in
        _kv_rows(y, kg[...], wdkv, kgl[...], wuk, wuvt, cos[...], sin[...], *post_out)


def _layer(h, g, ffn_w, layer, pre="plain", pre_args=(), post="plain", post_args=(), with_meta=True):
    n_tiles = NT if with_meta else NT_REAL
    row_spec = pl.BlockSpec((TM, D), lambda i: (i, 0))
    vec = lambda v: v.reshape(1, -1)

    args, in_specs = [h], [row_spec]
    if pre == "first":
        in_specs[0] = pl.BlockSpec((TM, D), lambda i: (jnp.minimum(i, NT_REAL - 1), 0))
        args += list(pre_args)
        in_specs += [_const_spec((TM, D))]
    elif pre == "pool":
        def halo_map(i):
            first = (i % TILES_PER_SEQ) == 0
            return (jnp.where(first, META_ROW0 // HALO, i * (TM // HALO) - 1), 0)
        pg, pw, ps = pre_args
        args += [h, vec(pg), pw, vec(ps)]
        in_specs += [pl.BlockSpec((HALO, D), halo_map), _const_spec((1, D)),
                     _const_spec((len(POOL_WINDOWS), POOL_GROUP, POOL_GROUP)), _const_spec((1, D))]
    elif pre == "wo":
        args += list(pre_args)
        in_specs += [pl.BlockSpec((H * V_HEAD, TM), lambda i: (0, i)), _const_spec((H * V_HEAD, D))]

    args += [vec(g), *ffn_w]
    in_specs += [_const_spec((1, D)), _layer_spec((D, D_FF), layer), _layer_spec((D, D_FF), layer),
                 _layer_spec((D_FF, D), layer)]

    out_shape = [jax.ShapeDtypeStruct((n_tiles * TM, D), F32)]
    out_specs = [row_spec]
    if post == "final":
        args += [vec(post_args[0])]
        in_specs += [_const_spec((1, D))]
    elif post == "q":
        qg, wdq, qgl, wqt, wqrt, cos_t, sin_t = post_args
        tab_spec = pl.BlockSpec((QK_ROPE, TM), lambda i: (0, _tab_tile(i)))
        args += [vec(qg), wdq, vec(qgl), wqt, wqrt, cos_t, sin_t]
        in_specs += [_const_spec((1, D)), _const_spec((D, Q_RANK)), _const_spec((1, Q_RANK)),
                     _const_spec((H * LANES, Q_RANK)), _const_spec((H * QK_ROPE, Q_RANK)),
                     tab_spec, tab_spec]
        out_shape += [jax.ShapeDtypeStruct((H, LANES, n_tiles * TM), BF16)]
        out_specs += [pl.BlockSpec((H, LANES, TM), lambda i: (0, 0, i))]
    elif post == "kv":
        kg, wdkv, kgl, wuk, wuvt, cos, sin = post_args
        tab_spec = pl.BlockSpec((TM, LANES), lambda i: (_tab_tile(i), 0))
        args += [vec(kg), wdkv, vec(kgl), wuk, wuvt, cos, sin]
        in_specs += [_const_spec((1, D)), _const_spec((D, KV_RANK + 2 * LANES)),
                     _const_spec((1, KV_RANK)), _const_spec((KV_RANK, H * LANES)),
                     _const_spec((H * V_HEAD, KV_RANK)), tab_spec, tab_spec]
        out_shape += [jax.ShapeDtypeStruct((H, ROWS, LANES), BF16),
                      jax.ShapeDtypeStruct((H * VR, ROWS), BF16)]
        out_specs += [pl.BlockSpec((H, TM, LANES), lambda i: (0, i, 0)),
                      pl.BlockSpec((H * VR, TM), lambda i: (0, i))]

    outs = pl.pallas_call(
        functools.partial(_layer_kernel, pre=pre, post=post),
        grid=(n_tiles,),
        in_specs=in_specs,
        out_specs=out_specs,
        out_shape=out_shape,
        scratch_shapes=[pltpu.VMEM((TM, D), BF16), pltpu.VMEM((TM, D_FF), BF16)],
        compiler_params=_params(),
        name=f"layer_{pre}_{post}",
    )(*args)
    return outs[0] if len(outs) == 1 else outs


def _attn_kernel(qt_ref, k_ref, vt_ref, km_ref, vmt_ref, o_ref, m_ref, acc_ref, s_ref):
    j = pl.program_id(1)

    def head_rows(hh):
        return slice(hh * VR, (hh + 1) * VR)

    def keys_of(hh, u, kb0):
        if u == META:
            return km_ref[hh], vmt_ref[head_rows(hh), :N_META]
        off = pl.multiple_of((kb0 + u) * KB, KB)
        return k_ref[hh, pl.ds(off, KB), :], vt_ref[head_rows(hh), pl.ds(off, KB)]

    def qk(hh, c, u, kb0):
        return jnp.dot(keys_of(hh, u, kb0)[0], qt_ref[hh, :, c * KB:(c + 1) * KB],
                       preferred_element_type=F32)

    kpos = lax.broadcasted_iota(jnp.int32, (KB, KB), 0)
    qpos = lax.broadcasted_iota(jnp.int32, (KB, KB), 1)
    chunk_mask = (kpos // CHUNK) <= (qpos // CHUNK)

    def run_tiles(tiles, kb0, tail):
        chains = sorted({(hh, c) for hh, c, _, _ in tiles})
        state = {}
        for hh, c in chains:
            cs = slice(c * KB, (c + 1) * KB)
            state[hh, c] = (m_ref[hh, :, cs], acc_ref[head_rows(hh), cs])
        assert len(tiles) % RING == 0
        pending = [(t, (hh, c, u, kb0)) for t, (hh, c, u, _) in enumerate(tiles)][LOOKAHEAD:]
        pending += [(len(tiles) + i, a) for i, a in enumerate(tail)]

        def issue():
            if pending:
                t, a = pending.pop(0)
                rows = N_META if a[2] == META else KB
                s_ref[t % RING, :rows, :] = qk(*a)

        def read(t, u, masked):
            if u == META:
                return s_ref[t % RING, :N_META, :]
            s = s_ref[t % RING]
            return jnp.where(chunk_mask, s, NEG) if masked else s

        for t, (hh, c, u, masked) in enumerate(tiles):
            m_old, acc = state[hh, c]
            m_new = jnp.maximum(m_old, jnp.max(read(t, u, masked), axis=0, keepdims=True))
            alpha = jnp.exp2(m_old - m_new)
            p = jnp.exp2((read(t, u, masked) - m_new).astype(BF16))
            issue()
            acc = alpha * acc + jnp.dot(keys_of(hh, u, kb0)[1], p, preferred_element_type=F32)
            state[hh, c] = (m_new, acc)
        for hh, c in chains:
            cs = slice(c * KB, (c + 1) * KB)
            m_ref[hh, :, cs], acc_ref[head_rows(hh), cs] = state[hh, c]

    full_tiles = [(hh, c, u, False) for u in range(NS) for hh in range(HP) for c in range(NS)]
    diag_tiles = [(hh, c, d, c == d) for d in range(NS) for hh in range(HP) for c in range(d, NS)]
    meta_tiles = [(hh, c, META, True) for hh in range(HP) for c in range(NS)]
    assert [t[:3] for t in full_tiles[:LOOKAHEAD]] == [t[:3] for t in diag_tiles[:LOOKAHEAD]]

    def first_scores(kb0):
        return [(hh, c, u, kb0) for hh, c, u, _ in full_tiles[:LOOKAHEAD]]

    q_tile = j % TILES_PER_SEQ

    m_ref[...] = jnp.full(m_ref.shape, NEG, F32)
    acc_ref[...] = jnp.zeros(acc_ref.shape, F32)
    for t, a in enumerate(first_scores(0)):
        s_ref[t] = qk(*a)

    pair_tiles = [(hh, c, u, False) for u in range(2 * NS) for hh in range(HP) for c in range(NS)]

    def pair_body(it, carry):
        run_tiles(pair_tiles, it * 2 * NS, first_scores((it + 1) * 2 * NS))
        return carry

    lax.fori_loop(0, q_tile // 2, pair_body, 0)

    @pl.when(q_tile % 2 == 1)
    def _():
        run_tiles(full_tiles, (q_tile - 1) * NS, first_scores(q_tile * NS))

    run_tiles(diag_tiles + meta_tiles, q_tile * NS, [])

    for hh in range(HP):
        acc = acc_ref[head_rows(hh), :]
        o_ref[hh * V_HEAD:(hh + 1) * V_HEAD, :] = (
            acc[:V_HEAD, :] / acc[V_HEAD:V_HEAD + 1, :]).astype(BF16)


def _attn(qt_all, k_all, vt_all):
    def b_of(j):
        return j // TILES_PER_SEQ

    return pl.pallas_call(
        _attn_kernel,
        grid=(H // HP, NT_REAL),
        in_specs=[pl.BlockSpec((HP, LANES, BQ), lambda h, j: (h, 0, j)),
                  pl.BlockSpec((HP, SEQ, LANES), lambda h, j: (h, b_of(j), 0),
                               pipeline_mode=pl.Buffered(1)),
                  pl.BlockSpec((HP * VR, SEQ), lambda h, j: (h, b_of(j)),
                               pipeline_mode=pl.Buffered(1)),
                  pl.BlockSpec((HP, N_META, LANES), lambda h, j: (h, META_ROW0 // N_META, 0)),
                  pl.BlockSpec((HP * VR, LANES), lambda h, j: (h, META_ROW0 // LANES))],
        out_specs=pl.BlockSpec((HP * V_HEAD, BQ), lambda h, j: (h, j)),
        out_shape=jax.ShapeDtypeStruct((H * V_HEAD, B * SEQ), BF16),
        scratch_shapes=[pltpu.VMEM((HP, 1, BQ), F32), pltpu.VMEM((HP * VR, BQ), F32),
                        pltpu.VMEM((RING, KB, KB), F32)],
        compiler_params=_params(2),
        name="attn",
    )(qt_all, k_all, vt_all, k_all, vt_all)


def _rot_cols(w):
    half = QK_ROPE // 2
    return jnp.concatenate([-w[..., half:], w[..., :half]], axis=-1)


def _place_rope(w):
    pad = [(0, 0)] * (w.ndim - 1) + [(QK_NOPE, LANES - QK_NOPE - QK_ROPE)]
    return jnp.pad(w, pad)


def _rope_tables():
    inv = 1.0 / (ROPE_THETA ** (jnp.arange(0, QK_ROPE, 2, dtype=F32) / QK_ROPE))
    row = jnp.arange(TAB_ROWS)
    pos = jnp.where(row < SEQ, N_META + row, row - SEQ).astype(F32)
    ang = pos[:, None] * inv[None, :]
    ones = jnp.ones((TAB_ROWS, QK_NOPE), F32)
    ones_hi = jnp.ones((TAB_ROWS, LANES - QK_NOPE - QK_ROPE), F32)
    cos_t = jnp.concatenate([ones, jnp.cos(ang), jnp.cos(ang), ones_hi], axis=1)
    sin_t = jnp.concatenate([0 * ones, jnp.sin(ang), jnp.sin(ang), 0 * ones_hi], axis=1)
    return cos_t, sin_t


def kernel(x, meta_tokens, ffn1_norm, ffn1_w_gate, ffn1_w_up, ffn1_w_down, mix_norm, ffn2_norm, ffn2_w_gate, ffn2_w_up, ffn2_w_down, pool_w, pool_scale, kv_in_norm, w_dkv, kv_latent_norm, w_uk, w_uv, w_dq, q_latent_norm, w_uq, w_o, final_norm):
    cos_t, sin_t = _rope_tables()
    frames = x.reshape(B * SEQ, D)
    meta_tile = jnp.concatenate(
        [meta_tokens.astype(x.dtype), jnp.zeros((TM - N_META, D), x.dtype)], axis=0)
    ffn1_w = [ffn1_w_gate, ffn1_w_up, ffn1_w_down]
    ffn2_w = [ffn2_w_gate, ffn2_w_up, ffn2_w_down]

    w_kr = w_dkv[:, KV_RANK:]
    wd_kv = jnp.concatenate(
        [w_dkv[:, :KV_RANK], _place_rope(w_kr), _place_rope(_rot_cols(w_kr))], axis=1).astype(BF16)
    wuk = jnp.pad(w_uk.reshape(KV_RANK, H, QK_NOPE),
                  ((0, 0), (0, 0), (0, LANES - QK_NOPE))).reshape(KV_RANK, H * LANES).astype(BF16)
    kv_args = (kv_in_norm, wd_kv, kv_latent_norm, wuk, w_uv.T.astype(BF16), cos_t, sin_t)

    def q_args(l):
        jj = l - N_A
        wq3 = w_uq[jj].reshape(Q_RANK, H, QK_NOPE + QK_ROPE)
        wq = jnp.pad(wq3, ((0, 0), (0, 0), (0, LANES - QK_NOPE - QK_ROPE)))
        wqr = _rot_cols(wq3[:, :, QK_NOPE:])
        to_t = lambda w: w.reshape(Q_RANK, -1).T.astype(BF16)
        rope_rows = slice(QK_NOPE, QK_NOPE + QK_ROPE)
        return (mix_norm[l], w_dq[jj].astype(BF16), q_latent_norm[jj], to_t(wq), to_t(wqr),
                cos_t[:, rope_rows].T, sin_t[:, rope_rows].T)

    h = _layer(frames, ffn1_norm[0], ffn1_w, 0, pre="first", pre_args=(meta_tile,))
    h = _layer(h, ffn2_norm[0], ffn2_w, 0, pre="pool",
               pre_args=(mix_norm[0], pool_w[0].astype(BF16), pool_scale[0]))
    h = _layer(h, ffn1_norm[1], ffn1_w, 1)
    h, k_all, vt_all = _layer(h, ffn2_norm[1], ffn2_w, 1, pre="pool",
                              pre_args=(mix_norm[1], pool_w[1].astype(BF16), pool_scale[1]),
                              post="kv", post_args=kv_args)
    for l in range(N_A, DEPTH):
        h, qt_all = _layer(h, ffn1_norm[l], ffn1_w, l, post="q", post_args=q_args(l),
                           with_meta=False)
        ot = _attn(qt_all, k_all, vt_all)
        last = l == DEPTH - 1
        h = _layer(h, ffn2_norm[l], ffn2_w, l, pre="wo", pre_args=(ot, w_o[l - N_A].astype(BF16)),
                   post="final" if last else "plain", post_args=(final_norm,) if last else (),
                   with_meta=False)
    return h.reshape(B, SEQ, D)
```

```python
import functools
import math

import jax
import jax.numpy as jnp
from jax import lax
from jax.experimental import pallas as pl
from jax.experimental.pallas import tpu as pltpu

D = 1024
B = 2
SEQ = 8192
DEPTH = 4
CHUNK = 64
N_META = 16
N_A = DEPTH // 2
D_FF = 2816
POOL_WINDOWS = (2, 4, 8, 16)
POOL_GROUP = D // len(POOL_WINDOWS)
H = 8
QK_NOPE = 64
QK_ROPE = 32
V_HEAD = 64
KV_RANK = 256
Q_RANK = 384
ROPE_THETA = 10000.0
EPS = 1e-6

F32 = jnp.float32
BF16 = jnp.bfloat16

LANES = 128
TM = 512
NT_REAL = B * SEQ // TM
NT = NT_REAL + 1
ROWS = NT * TM
TILES_PER_SEQ = SEQ // TM
META_ROW0 = B * SEQ
TAB_ROWS = SEQ + TM
FC = 256
KB = 256
BQ = TM
NS = BQ // KB
HP = H
RING = 8
LOOKAHEAD = 6
HALO = 16
NEG = -1e30
META = -1
VR = V_HEAD + 16
VMEM_LIMIT = 56 * 1024 * 1024
NT_DIMS = (((1,), (1,)), ((), ()))


def _inv_rms(x):
    return lax.rsqrt(jnp.mean(x * x, axis=-1, keepdims=True) + EPS)


def _rms(x, g):
    return (x * _inv_rms(x)) * g


def _rms_dot(x, g, w, n_out):
    r = jnp.broadcast_to(_inv_rms(x), (x.shape[0], n_out))
    return r * jnp.dot((x * g).astype(BF16), w, preferred_element_type=F32)


def _params(n_axes=1):
    return pltpu.CompilerParams(
        dimension_semantics=("arbitrary",) * n_axes, vmem_limit_bytes=VMEM_LIMIT)


def _const_spec(shape):
    zeros = (0,) * len(shape)
    return pl.BlockSpec(shape, lambda i: zeros, pipeline_mode=pl.Buffered(1))


def _layer_spec(shape, layer):
    return pl.BlockSpec((None,) + shape, lambda i: (layer, 0, 0), pipeline_mode=pl.Buffered(1))


def _tab_tile(i):
    return jnp.where(i < NT_REAL, i % TILES_PER_SEQ, TILES_PER_SEQ)


def _pool_rows(x, halo, is_meta, g, w_ref, scale):
    u = _rms(x, g)
    uh = _rms(halo, g) * jnp.where(is_meta, 0.0, 1.0)
    n_avail = lax.broadcasted_iota(jnp.int32, (HALO, POOL_GROUP), 0) + jnp.where(is_meta, 1, HALO)
    pad = jnp.zeros((HALO, POOL_GROUP), F32)
    outs = []
    for gi, w in enumerate(POOL_WINDOWS):
        sl = slice(gi * POOL_GROUP, (gi + 1) * POOL_GROUP)
        ug = u[:, sl]
        acc = jnp.concatenate([pad, uh[:, sl], ug], axis=0)
        step = 1
        while step < w:
            acc = acc + pltpu.roll(acc, step, 0)
            step *= 2
        head = acc[2 * HALO:3 * HALO, :] / jnp.minimum(n_avail, w).astype(F32)
        pooled = jnp.concatenate([head, acc[3 * HALO:, :] * (1.0 / w)], axis=0) - ug
        y = jnp.dot(pooled.astype(BF16), w_ref[gi], preferred_element_type=F32)
        outs.append(x[:, sl] + y * scale[:, sl])
    return jnp.concatenate(outs, axis=1)


def _q_rows(y, g, wdq_ref, gl, wqt_ref, wqrt_ref, cos_t, sin_t, q_ref):
    cq = _rms_dot(y, g, wdq_ref[...], Q_RANK)
    cqn = _rms(cq, gl).astype(BF16)
    qa = lax.dot_general(wqt_ref[...], cqn, NT_DIMS, preferred_element_type=F32)
    qb = lax.dot_general(wqrt_ref[...], cqn, NT_DIMS, preferred_element_type=F32)
    scale = (1.0 / math.sqrt(QK_NOPE + QK_ROPE)) * math.log2(math.e)
    lo, hi = QK_NOPE, QK_NOPE + QK_ROPE
    for hh in range(H):
        a = qa[hh * LANES:(hh + 1) * LANES, :]
        rope = a[lo:hi, :] * cos_t + qb[hh * QK_ROPE:(hh + 1) * QK_ROPE, :] * sin_t
        q_ref[hh] = (jnp.concatenate([a[:lo, :], rope, a[hi:, :]], axis=0) * scale).astype(BF16)


def _kv_rows(y, g, wd_ref, gl, wuk_ref, wuvt_ref, cos, sin, k_ref, vt_ref):
    ckr = _rms_dot(y, g, wd_ref[...], KV_RANK + 2 * LANES)
    ckv = _rms(ckr[:, :KV_RANK], gl).astype(BF16)
    kr = ckr[:, KV_RANK:KV_RANK + LANES] * cos + ckr[:, KV_RANK + LANES:] * sin
    kn = jnp.dot(ckv, wuk_ref[...], preferred_element_type=F32)
    for hh in range(H):
        k_ref[hh] = (kn[:, hh * LANES:(hh + 1) * LANES] + kr).astype(BF16)
    vt = lax.dot_general(wuvt_ref[...], ckv, NT_DIMS, preferred_element_type=F32)
    ones_row = (lax.broadcasted_iota(jnp.int32, (VR - V_HEAD, TM), 0) == 0).astype(BF16)
    for hh in range(H):
        vt_ref[hh * VR:hh * VR + V_HEAD, :] = vt[hh * V_HEAD:(hh + 1) * V_HEAD, :].astype(BF16)
        vt_ref[hh * VR + V_HEAD:(hh + 1) * VR, :] = ones_row


def _layer_kernel(*refs, pre, post):
    refs = list(refs)

    def take(n):
        return [refs.pop(0) for _ in range(n)]

    i = pl.program_id(0)
    is_meta = i == NT - 1
    (x_ref,) = take(1)
    x = x_ref[...]
    if pre == "first":
        (xm_ref,) = take(1)
        x = jnp.where(is_meta, xm_ref[...], x)
    elif pre == "pool":
        halo_ref, pg_ref, pw_ref, ps_ref = take(4)
        x = _pool_rows(x, halo_ref[...], is_meta, pg_ref[...], pw_ref, ps_ref[...])
    elif pre == "wo":
        ot_ref, wo_ref = take(2)
        x = x + lax.dot_general(ot_ref[...], wo_ref[...], (((0,), (0,)), ((), ())),
                                preferred_element_type=F32)
    g_ref, wg_ref, wu_ref, wd_ref = take(4)
    post_in = take({"plain": 0, "final": 1, "q": 7, "kv": 7}[post])
    (o_ref,) = take(1)
    post_out = take({"plain": 0, "final": 0, "q": 1, "kv": 2}[post])
    xn_ref, a_ref = take(2)
    assert not refs

    xn_ref[...] = (x * g_ref[...]).astype(BF16)
    r = jnp.broadcast_to(_inv_rms(x), (TM, FC))
    for c in range(D_FF // FC):
        sl = slice(c * FC, (c + 1) * FC)
        gate = r * jnp.dot(xn_ref[...], wg_ref[:, sl].astype(BF16), preferred_element_type=F32)
        up = r * jnp.dot(xn_ref[...], wu_ref[:, sl].astype(BF16), preferred_element_type=F32)
        a_ref[:, sl] = (gate * jax.nn.sigmoid(gate) * up).astype(BF16)
    y = jnp.concatenate(
        [x[:, n * FC:(n + 1) * FC] + 0.5 * jnp.dot(
            a_ref[...], wd_ref[:, n * FC:(n + 1) * FC].astype(BF16), preferred_element_type=F32)
         for n in range(D // FC)], axis=1)

    if post == "final":
        y = _rms(y, post_in[0][...])
    o_ref[...] = y
    if post == "q":
        qg, wdq, qgl, wqt, wqrt, cos_t, sin_t = post_in
        _q_rows(y, qg[...], wdq, qgl[...], wqt, wqrt, cos_t[...], sin_t[...], post_out[0])
    elif post == "kv":
        kg, wdkv, kgl, wuk, wuvt, cos, sin = post_in
        _kv_rows(y, kg[...], wdkv, kgl[...], wuk, wuvt, cos[...], sin[...], *post_out)


def _layer(h, g, ffn_w, layer, pre="plain", pre_args=(), post="plain", post_args=(), with_meta=True):
    n_tiles = NT if with_meta else NT_REAL
    row_spec = pl.BlockSpec((TM, D), lambda i: (i, 0))
    vec = lambda v: v.reshape(1, -1)

    args, in_specs = [h], [row_spec]
    if pre == "first":
        in_specs[0] = pl.BlockSpec((TM, D), lambda i: (jnp.minimum(i, NT_REAL - 1), 0))
        args += list(pre_args)
        in_specs += [_const_spec((TM, D))]
    elif pre == "pool":
        def halo_map(i):
            first = (i % TILES_PER_SEQ) == 0
            return (jnp.where(first, META_ROW0 // HALO, i * (TM // HALO) - 1), 0)
        pg, pw, ps = pre_args
        args += [h, vec(pg), pw, vec(ps)]
        in_specs += [pl.BlockSpec((HALO, D), halo_map), _const_spec((1, D)),
                     _const_spec((len(POOL_WINDOWS), POOL_GROUP, POOL_GROUP)), _const_spec((1, D))]
    elif pre == "wo":
        args += list(pre_args)
        in_specs += [pl.BlockSpec((H * V_HEAD, TM), lambda i: (0, i)), _const_spec((H * V_HEAD, D))]

    args += [vec(g), *ffn_w]
    in_specs += [_const_spec((1, D)), _layer_spec((D, D_FF), layer), _layer_spec((D, D_FF), layer),
                 _layer_spec((D_FF, D), layer)]

    out_shape = [jax.ShapeDtypeStruct((n_tiles * TM, D), F32)]
    out_specs = [row_spec]
    if post == "final":
        args += [vec(post_args[0])]
        in_specs += [_const_spec((1, D))]
    elif post == "q":
        qg, wdq, qgl, wqt, wqrt, cos_t, sin_t = post_args
        tab_spec = pl.BlockSpec((QK_ROPE, TM), lambda i: (0, _tab_tile(i)))
        args += [vec(qg), wdq, vec(qgl), wqt, wqrt, cos_t, sin_t]
        in_specs += [_const_spec((1, D)), _const_spec((D, Q_RANK)), _const_spec((1, Q_RANK)),
                     _const_spec((H * LANES, Q_RANK)), _const_spec((H * QK_ROPE, Q_RANK)),
                     tab_spec, tab_spec]
        out_shape += [jax.ShapeDtypeStruct((H, LANES, n_tiles * TM), BF16)]
        out_specs += [pl.BlockSpec((H, LANES, TM), lambda i: (0, 0, i))]
    elif post == "kv":
        kg, wdkv, kgl, wuk, wuvt, cos, sin = post_args
        tab_spec = pl.BlockSpec((TM, LANES), lambda i: (_tab_tile(i), 0))
        args += [vec(kg), wdkv, vec(kgl), wuk, wuvt, cos, sin]
        in_specs += [_const_spec((1, D)), _const_spec((D, KV_RANK + 2 * LANES)),
                     _const_spec((1, KV_RANK)), _const_spec((KV_RANK, H * LANES)),
                     _const_spec((H * V_HEAD, KV_RANK)), tab_spec, tab_spec]
        out_shape += [jax.ShapeDtypeStruct((H, ROWS, LANES), BF16),
                      jax.ShapeDtypeStruct((H * VR, ROWS), BF16)]
        out_specs += [pl.BlockSpec((H, TM, LANES), lambda i: (0, i, 0)),
                      pl.BlockSpec((H * VR, TM), lambda i: (0, i))]

    outs = pl.pallas_call(
        functools.partial(_layer_kernel, pre=pre, post=post),
        grid=(n_tiles,),
        in_specs=in_specs,
        out_specs=out_specs,
        out_shape=out_shape,
        scratch_shapes=[pltpu.VMEM((TM, D), BF16), pltpu.VMEM((TM, D_FF), BF16)],
        compiler_params=_params(),
        name=f"layer_{pre}_{post}",
    )(*args)
    return outs[0] if len(outs) == 1 else outs


def _attn_kernel(qt_ref, k_ref, vt_ref, km_ref, vmt_ref, o_ref, m_ref, acc_ref, s_ref):
    j = pl.program_id(1)

    def head_rows(hh):
        return slice(hh * VR, (hh + 1) * VR)

    def keys_of(hh, u, kb0):
        if u == META:
            return km_ref[hh], vmt_ref[head_rows(hh), :N_META]
        off = pl.multiple_of((kb0 + u) * KB, KB)
        return k_ref[hh, pl.ds(off, KB), :], vt_ref[head_rows(hh), pl.ds(off, KB)]

    def qk(hh, c, u, kb0):
        return jnp.dot(keys_of(hh, u, kb0)[0], qt_ref[hh, :, c * KB:(c + 1) * KB],
                       preferred_element_type=F32)

    kpos = lax.broadcasted_iota(jnp.int32, (KB, KB), 0)
    qpos = lax.broadcasted_iota(jnp.int32, (KB, KB), 1)
    chunk_mask = (kpos // CHUNK) <= (qpos // CHUNK)

    def run_tiles(tiles, kb0, tail):
        chains = sorted({(hh, c) for hh, c, _, _ in tiles})
        state = {}
        for hh, c in chains:
            cs = slice(c * KB, (c + 1) * KB)
            state[hh, c] = (m_ref[hh, :, cs], acc_ref[head_rows(hh), cs])
        assert len(tiles) % RING == 0
        pending = [(t, (hh, c, u, kb0)) for t, (hh, c, u, _) in enumerate(tiles)][LOOKAHEAD:]
        pending += [(len(tiles) + i, a) for i, a in enumerate(tail)]

        def issue():
            if pending:
                t, a = pending.pop(0)
                rows = N_META if a[2] == META else KB
                s_ref[t % RING, :rows, :] = qk(*a)

        def read(t, u, masked):
            if u == META:
                return s_ref[t % RING, :N_META, :]
            s = s_ref[t % RING]
            return jnp.where(chunk_mask, s, NEG) if masked else s

        for t, (hh, c, u, masked) in enumerate(tiles):
            m_old, acc = state[hh, c]
            m_new = jnp.maximum(m_old, jnp.max(read(t, u, masked), axis=0, keepdims=True))
            alpha = jnp.exp2(m_old - m_new)
            p = jnp.exp2((read(t, u, masked) - m_new).astype(BF16))
            issue()
            acc = alpha * acc + jnp.dot(keys_of(hh, u, kb0)[1], p, preferred_element_type=F32)
            state[hh, c] = (m_new, acc)
        for hh, c in chains:
            cs = slice(c * KB, (c + 1) * KB)
            m_ref[hh, :, cs], acc_ref[head_rows(hh), cs] = state[hh, c]

    full_tiles = [(hh, c, u, False) for u in range(NS) for hh in range(HP) for c in range(NS)]
    diag_tiles = [(hh, c, d, c == d) for d in range(NS) for hh in range(HP) for c in range(d, NS)]
    meta_tiles = [(hh, c, META, True) for hh in range(HP) for c in range(NS)]
    assert [t[:3] for t in full_tiles[:LOOKAHEAD]] == [t[:3] for t in diag_tiles[:LOOKAHEAD]]

    def first_scores(kb0):
        return [(hh, c, u, kb0) for hh, c, u, _ in full_tiles[:LOOKAHEAD]]

    q_tile = j % TILES_PER_SEQ

    m_ref[...] = jnp.full(m_ref.shape, NEG, F32)
    acc_ref[...] = jnp.zeros(acc_ref.shape, F32)
    for t, a in enumerate(first_scores(0)):
        s_ref[t] = qk(*a)

    pair_tiles = [(hh, c, u, False) for u in range(2 * NS) for hh in range(HP) for c in range(NS)]

    def pair_body(it, carry):
        run_tiles(pair_tiles, it * 2 * NS, first_scores((it + 1) * 2 * NS))
        return carry

    lax.fori_loop(0, q_tile // 2, pair_body, 0)

    @pl.when(q_tile % 2 == 1)
    def _():
        run_tiles(full_tiles, (q_tile - 1) * NS, first_scores(q_tile * NS))

    run_tiles(diag_tiles + meta_tiles, q_tile * NS, [])

    for hh in range(HP):
        acc = acc_ref[head_rows(hh), :]
        o_ref[hh * V_HEAD:(hh + 1) * V_HEAD, :] = (
            acc[:V_HEAD, :] / acc[V_HEAD:V_HEAD + 1, :]).astype(BF16)


def _attn(qt_all, k_all, vt_all):
    def b_of(j):
        return j // TILES_PER_SEQ

    return pl.pallas_call(
        _attn_kernel,
        grid=(H // HP, NT_REAL),
        in_specs=[pl.BlockSpec((HP, LANES, BQ), lambda h, j: (h, 0, j)),
                  pl.BlockSpec((HP, SEQ, LANES), lambda h, j: (h, b_of(j), 0),
                               pipeline_mode=pl.Buffered(1)),
                  pl.BlockSpec((HP * VR, SEQ), lambda h, j: (h, b_of(j)),
                               pipeline_mode=pl.Buffered(1)),
                  pl.BlockSpec((HP, N_META, LANES), lambda h, j: (h, META_ROW0 // N_META, 0)),
                  pl.BlockSpec((HP * VR, LANES), lambda h, j: (h, META_ROW0 // LANES))],
        out_specs=pl.BlockSpec((HP * V_HEAD, BQ), lambda h, j: (h, j)),
        out_shape=jax.ShapeDtypeStruct((H * V_HEAD, B * SEQ), BF16),
        scratch_shapes=[pltpu.VMEM((HP, 1, BQ), F32), pltpu.VMEM((HP * VR, BQ), F32),
                        pltpu.VMEM((RING, KB, KB), F32)],
        compiler_params=_params(2),
        name="attn",
    )(qt_all, k_all, vt_all, k_all, vt_all)


def _rot_cols(w):
    half = QK_ROPE // 2
    return jnp.concatenate([-w[..., half:], w[..., :half]], axis=-1)


def _place_rope(w):
    pad = [(0, 0)] * (w.ndim - 1) + [(QK_NOPE, LANES - QK_NOPE - QK_ROPE)]
    return jnp.pad(w, pad)


def _rope_tables():
    inv = 1.0 / (ROPE_THETA ** (jnp.arange(0, QK_ROPE, 2, dtype=F32) / QK_ROPE))
    row = jnp.arange(TAB_ROWS)
    pos = jnp.where(row < SEQ, N_META + row, row - SEQ).astype(F32)
    ang = pos[:, None] * inv[None, :]
    ones = jnp.ones((TAB_ROWS, QK_NOPE), F32)
    ones_hi = jnp.ones((TAB_ROWS, LANES - QK_NOPE - QK_ROPE), F32)
    cos_t = jnp.concatenate([ones, jnp.cos(ang), jnp.cos(ang), ones_hi], axis=1)
    sin_t = jnp.concatenate([0 * ones, jnp.sin(ang), jnp.sin(ang), 0 * ones_hi], axis=1)
    return cos_t, sin_t


def kernel(x, meta_tokens, ffn1_norm, ffn1_w_gate, ffn1_w_up, ffn1_w_down, mix_norm, ffn2_norm, ffn2_w_gate, ffn2_w_up, ffn2_w_down, pool_w, pool_scale, kv_in_norm, w_dkv, kv_latent_norm, w_uk, w_uv, w_dq, q_latent_norm, w_uq, w_o, final_norm):
    cos_t, sin_t = _rope_tables()
    frames = x.reshape(B * SEQ, D)
    meta_tile = jnp.concatenate(
        [meta_tokens.astype(x.dtype), jnp.zeros((TM - N_META, D), x.dtype)], axis=0)
    ffn1_w = [ffn1_w_gate, ffn1_w_up, ffn1_w_down]
    ffn2_w = [ffn2_w_gate, ffn2_w_up, ffn2_w_down]

    w_kr = w_dkv[:, KV_RANK:]
    wd_kv = jnp.concatenate(
        [w_dkv[:, :KV_RANK], _place_rope(w_kr), _place_rope(_rot_cols(w_kr))], axis=1).astype(BF16)
    wuk = jnp.pad(w_uk.reshape(KV_RANK, H, QK_NOPE),
                  ((0, 0), (0, 0), (0, LANES - QK_NOPE))).reshape(KV_RANK, H * LANES).astype(BF16)
    kv_args = (kv_in_norm, wd_kv, kv_latent_norm, wuk, w_uv.T.astype(BF16), cos_t, sin_t)

    def q_args(l):
        jj = l - N_A
        wq3 = w_uq[jj].reshape(Q_RANK, H, QK_NOPE + QK_ROPE)
        wq = jnp.pad(wq3, ((0, 0), (0, 0), (0, LANES - QK_NOPE - QK_ROPE)))
        wqr = _rot_cols(wq3[:, :, QK_NOPE:])
        to_t = lambda w: w.reshape(Q_RANK, -1).T.astype(BF16)
        rope_rows = slice(QK_NOPE, QK_NOPE + QK_ROPE)
        return (mix_norm[l], w_dq[jj].astype(BF16), q_latent_norm[jj], to_t(wq), to_t(wqr),
                cos_t[:, rope_rows].T, sin_t[:, rope_rows].T)

    h = _layer(frames, ffn1_norm[0], ffn1_w, 0, pre="first", pre_args=(meta_tile,))
    h = _layer(h, ffn2_norm[0], ffn2_w, 0, pre="pool",
               pre_args=(mix_norm[0], pool_w[0].astype(BF16), pool_scale[0]))
    h = _layer(h, ffn1_norm[1], ffn1_w, 1)
    h, k_all, vt_all = _layer(h, ffn2_norm[1], ffn2_w, 1, pre="pool",
                              pre_args=(mix_norm[1], pool_w[1].astype(BF16), pool_scale[1]),
                              post="kv", post_args=kv_args)
    for l in range(N_A, DEPTH):
        h, qt_all = _layer(h, ffn1_norm[l], ffn1_w, l, post="q", post_args=q_args(l),
                           with_meta=False)
        ot = _attn(qt_all, k_all, vt_all)
        last = l == DEPTH - 1
        h = _layer(h, ffn2_norm[l], ffn2_w, l, pre="wo", pre_args=(ot, w_o[l - N_A].astype(BF16)),
                   post="final" if last else "plain", post_args=(final_norm,) if last else (),
                   with_meta=False)
    return h.reshape(B, SEQ, D)
```

```python
import functools
import math

import jax
import jax.numpy as jnp
from jax import lax
from jax.experimental import pallas as pl
from jax.experimental.pallas import tpu as pltpu

D = 1024
B = 2
SEQ = 8192
DEPTH = 4
CHUNK = 64
N_META = 16
N_A = DEPTH // 2
D_FF = 2816
POOL_WINDOWS = (2, 4, 8, 16)
POOL_GROUP = D // len(POOL_WINDOWS)
H = 8
QK_NOPE = 64
QK_ROPE = 32
V_HEAD = 64
KV_RANK = 256
Q_RANK = 384
ROPE_THETA = 10000.0
EPS = 1e-6

F32 = jnp.float32
BF16 = jnp.bfloat16

LANES = 128
TM = 512
NT_REAL = B * SEQ // TM
NT = NT_REAL + 1
ROWS = NT * TM
TILES_PER_SEQ = SEQ // TM
META_ROW0 = B * SEQ
TAB_ROWS = SEQ + TM
FC = 256
KB = 256
BQ = TM
NS = BQ // KB
HP = H
RING = 8
LOOKAHEAD = 6
HALO = 16
NEG = -1e30
META = -1
VR = V_HEAD + 16
VMEM_LIMIT = 56 * 1024 * 1024
NT_DIMS = (((1,), (1,)), ((), ()))


def _inv_rms(x):
    return lax.rsqrt(jnp.mean(x * x, axis=-1, keepdims=True) + EPS)


def _rms(x, g):
    return (x * _inv_rms(x)) * g


def _rms_dot(x, g, w, n_out):
    r = jnp.broadcast_to(_inv_rms(x), (x.shape[0], n_out))
    return r * jnp.dot((x * g).astype(BF16), w, preferred_element_type=F32)


def _params(n_axes=1):
    return pltpu.CompilerParams(
        dimension_semantics=("arbitrary",) * n_axes, vmem_limit_bytes=VMEM_LIMIT)


def _const_spec(shape):
    zeros = (0,) * len(shape)
    return pl.BlockSpec(shape, lambda i: zeros, pipeline_mode=pl.Buffered(1))


def _layer_spec(shape, layer):
    return pl.BlockSpec((None,) + shape, lambda i: (layer, 0, 0), pipeline_mode=pl.Buffered(1))


def _tab_tile(i):
    return jnp.where(i < NT_REAL, i % TILES_PER_SEQ, TILES_PER_SEQ)


def _pool_rows(x, halo, is_meta, g, w_ref, scale):
    u = _rms(x, g)
    uh = _rms(halo, g) * jnp.where(is_meta, 0.0, 1.0)
    n_avail = lax.broadcasted_iota(jnp.int32, (HALO, POOL_GROUP), 0) + jnp.where(is_meta, 1, HALO)
    pad = jnp.zeros((HALO, POOL_GROUP), F32)
    outs = []
    for gi, w in enumerate(POOL_WINDOWS):
        sl = slice(gi * POOL_GROUP, (gi + 1) * POOL_GROUP)
        ug = u[:, sl]
        acc = jnp.concatenate([pad, uh[:, sl], ug], axis=0)
        step = 1
        while step < w:
            acc = acc + pltpu.roll(acc, step, 0)
            step *= 2
        head = acc[2 * HALO:3 * HALO, :] / jnp.minimum(n_avail, w).astype(F32)
        pooled = jnp.concatenate([head, acc[3 * HALO:, :] * (1.0 / w)], axis=0) - ug
        y = jnp.dot(pooled.astype(BF16), w_ref[gi], preferred_element_type=F32)
        outs.append(x[:, sl] + y * scale[:, sl])
    return jnp.concatenate(outs, axis=1)


def _q_rows(y, g, wdq_ref, gl, wqt_ref, wqrt_ref, cos_t, sin_t, q_ref):
    cq = _rms_dot(y, g, wdq_ref[...], Q_RANK)
    cqn = _rms(cq, gl).astype(BF16)
    qa = lax.dot_general(wqt_ref[...], cqn, NT_DIMS, preferred_element_type=F32)
    qb = lax.dot_general(wqrt_ref[...], cqn, NT_DIMS, preferred_element_type=F32)
    scale = (1.0 / math.sqrt(QK_NOPE + QK_ROPE)) * math.log2(math.e)
    lo, hi = QK_NOPE, QK_NOPE + QK_ROPE
    for hh in range(H):
        a = qa[hh * LANES:(hh + 1) * LANES, :]
        rope = a[lo:hi, :] * cos_t + qb[hh * QK_ROPE:(hh + 1) * QK_ROPE, :] * sin_t
        q_ref[hh] = (jnp.concatenate([a[:lo, :], rope, a[hi:, :]], axis=0) * scale).astype(BF16)


def _kv_rows(y, g, wd_ref, gl, wuk_ref, wuvt_ref, cos, sin, k_ref, vt_ref):
    ckr = _rms_dot(y, g, wd_ref[...], KV_RANK + 2 * LANES)
    ckv = _rms(ckr[:, :KV_RANK], gl).astype(BF16)
    kr = ckr[:, KV_RANK:KV_RANK + LANES] * cos + ckr[:, KV_RANK + LANES:] * sin
    kn = jnp.dot(ckv, wuk_ref[...], preferred_element_type=F32)
    for hh in range(H):
        k_ref[hh] = (kn[:, hh * LANES:(hh + 1) * LANES] + kr).astype(BF16)
    vt = lax.dot_general(wuvt_ref[...], ckv, NT_DIMS, preferred_element_type=F32)
    ones_row = (lax.broadcasted_iota(jnp.int32, (VR - V_HEAD, TM), 0) == 0).astype(BF16)
    for hh in range(H):
        vt_ref[hh * VR:hh * VR + V_HEAD, :] = vt[hh * V_HEAD:(hh + 1) * V_HEAD, :].astype(BF16)
        vt_ref[hh * VR + V_HEAD:(hh + 1) * VR, :] = ones_row


def _layer_kernel(*refs, pre, post):
    refs = list(refs)

    def take(n):
        return [refs.pop(0) for _ in range(n)]

    i = pl.program_id(0)
    is_meta = i == NT - 1
    (x_ref,) = take(1)
    x = x_ref[...]
    if pre == "first":
        (xm_ref,) = take(1)
        x = jnp.where(is_meta, xm_ref[...], x)
    elif pre == "pool":
        halo_ref, pg_ref, pw_ref, ps_ref = take(4)
        x = _pool_rows(x, halo_ref[...], is_meta, pg_ref[...], pw_ref, ps_ref[...])
    elif pre == "wo":
        ot_ref, wo_ref = take(2)
        x = x + lax.dot_general(ot_ref[...], wo_ref[...], (((0,), (0,)), ((), ())),
                                preferred_element_type=F32)
    g_ref, wg_ref, wu_ref, wd_ref = take(4)
    post_in = take({"plain": 0, "final": 1, "q": 7, "kv": 7}[post])
    (o_ref,) = take(1)
    post_out = take({"plain": 0, "final": 0, "q": 1, "kv": 2}[post])
    xn_ref, a_ref = take(2)
    assert not refs

    xn_ref[...] = (x * g_ref[...]).astype(BF16)
    r = jnp.broadcast_to(_inv_rms(x), (TM, FC))
    y = x
    for c in range(D_FF // FC):
        sl = slice(c * FC, (c + 1) * FC)
        gate = r * jnp.dot(xn_ref[...], wg_ref[:, sl].astype(BF16), preferred_element_type=F32)
        up = r * jnp.dot(xn_ref[...], wu_ref[:, sl].astype(BF16), preferred_element_type=F32)
        a = (gate * jax.nn.sigmoid(gate) * up).astype(BF16)
        y = y + 0.5 * jnp.dot(a, wd_ref[sl, :].astype(BF16), preferred_element_type=F32)

    if post == "final":
        y = _rms(y, post_in[0][...])
    o_ref[...] = y
    if post == "q":
        qg, wdq, qgl, wqt, wqrt, cos_t, sin_t = post_in
        _q_rows(y, qg[...], wdq, qgl[...], wqt, wqrt, cos_t[...], sin_t[...], post_out[0])
    elif post == "kv":
        kg, wdkv, kgl, wuk, wuvt, cos, sin = post_in
        _kv_rows(y, kg[...], wdkv, kgl[...], wuk, wuvt, cos[...], sin[...], *post_out)


def _layer(h, g, ffn_w, layer, pre="plain", pre_args=(), post="plain", post_args=(), with_meta=True):
    n_tiles = NT if with_meta else NT_REAL
    row_spec = pl.BlockSpec((TM, D), lambda i: (i, 0))
    vec = lambda v: v.reshape(1, -1)

    args, in_specs = [h], [row_spec]
    if pre == "first":
        in_specs[0] = pl.BlockSpec((TM, D), lambda i: (jnp.minimum(i, NT_REAL - 1), 0))
        args += list(pre_args)
        in_specs += [_const_spec((TM, D))]
    elif pre == "pool":
        def halo_map(i):
            first = (i % TILES_PER_SEQ) == 0
            return (jnp.where(first, META_ROW0 // HALO, i * (TM // HALO) - 1), 0)
        pg, pw, ps = pre_args
        args += [h, vec(pg), pw, vec(ps)]
        in_specs += [pl.BlockSpec((HALO, D), halo_map), _const_spec((1, D)),
                     _const_spec((len(POOL_WINDOWS), POOL_GROUP, POOL_GROUP)), _const_spec((1, D))]
    elif pre == "wo":
        args += list(pre_args)
        in_specs += [pl.BlockSpec((H * V_HEAD, TM), lambda i: (0, i)), _const_spec((H * V_HEAD, D))]

    args += [vec(g), *ffn_w]
    in_specs += [_const_spec((1, D)), _layer_spec((D, D_FF), layer), _layer_spec((D, D_FF), layer),
                 _layer_spec((D_FF, D), layer)]

    out_shape = [jax.ShapeDtypeStruct((n_tiles * TM, D), F32)]
    out_specs = [row_spec]
    if post == "final":
        args += [vec(post_args[0])]
        in_specs += [_const_spec((1, D))]
    elif post == "q":
        qg, wdq, qgl, wqt, wqrt, cos_t, sin_t = post_args
        tab_spec = pl.BlockSpec((QK_ROPE, TM), lambda i: (0, _tab_tile(i)))
        args += [vec(qg), wdq, vec(qgl), wqt, wqrt, cos_t, sin_t]
        in_specs += [_const_spec((1, D)), _const_spec((D, Q_RANK)), _const_spec((1, Q_RANK)),
                     _const_spec((H * LANES, Q_RANK)), _const_spec((H * QK_ROPE, Q_RANK)),
                     tab_spec, tab_spec]
        out_shape += [jax.ShapeDtypeStruct((H, LANES, n_tiles * TM), BF16)]
        out_specs += [pl.BlockSpec((H, LANES, TM), lambda i: (0, 0, i))]
    elif post == "kv":
        kg, wdkv, kgl, wuk, wuvt, cos, sin = post_args
        tab_spec = pl.BlockSpec((TM, LANES), lambda i: (_tab_tile(i), 0))
        args += [vec(kg), wdkv, vec(kgl), wuk, wuvt, cos, sin]
        in_specs += [_const_spec((1, D)), _const_spec((D, KV_RANK + 2 * LANES)),
                     _const_spec((1, KV_RANK)), _const_spec((KV_RANK, H * LANES)),
                     _const_spec((H * V_HEAD, KV_RANK)), tab_spec, tab_spec]
        out_shape += [jax.ShapeDtypeStruct((H, ROWS, LANES), BF16),
                      jax.ShapeDtypeStruct((H * VR, ROWS), BF16)]
        out_specs += [pl.BlockSpec((H, TM, LANES), lambda i: (0, i, 0)),
                      pl.BlockSpec((H * VR, TM), lambda i: (0, i))]

    outs = pl.pallas_call(
        functools.partial(_layer_kernel, pre=pre, post=post),
        grid=(n_tiles,),
        in_specs=in_specs,
        out_specs=out_specs,
        out_shape=out_shape,
        scratch_shapes=[pltpu.VMEM((TM, D), BF16), pltpu.VMEM((TM, D_FF), BF16)],
        compiler_params=_params(),
        name=f"layer_{pre}_{post}",
    )(*args)
    return outs[0] if len(outs) == 1 else outs


def _attn_kernel(qt_ref, k_ref, vt_ref, km_ref, vmt_ref, o_ref, m_ref, acc_ref, s_ref):
    j = pl.program_id(1)

    def head_rows(hh):
        return slice(hh * VR, (hh + 1) * VR)

    def keys_of(hh, u, kb0):
        if u == META:
            return km_ref[hh], vmt_ref[head_rows(hh), :N_META]
        off = pl.multiple_of((kb0 + u) * KB, KB)
        return k_ref[hh, pl.ds(off, KB), :], vt_ref[head_rows(hh), pl.ds(off, KB)]

    def qk(hh, c, u, kb0):
        return jnp.dot(keys_of(hh, u, kb0)[0], qt_ref[hh, :, c * KB:(c + 1) * KB],
                       preferred_element_type=F32)

    kpos = lax.broadcasted_iota(jnp.int32, (KB, KB), 0)
    qpos = lax.broadcasted_iota(jnp.int32, (KB, KB), 1)
    chunk_mask = (kpos // CHUNK) <= (qpos // CHUNK)

    def run_tiles(tiles, kb0, tail):
        chains = sorted({(hh, c) for hh, c, _, _ in tiles})
        state = {}
        for hh, c in chains:
            cs = slice(c * KB, (c + 1) * KB)
            state[hh, c] = (m_ref[hh, :, cs], acc_ref[head_rows(hh), cs])
        assert len(tiles) % RING == 0
        pending = [(t, (hh, c, u, kb0)) for t, (hh, c, u, _) in enumerate(tiles)][LOOKAHEAD:]
        pending += [(len(tiles) + i, a) for i, a in enumerate(tail)]

        def issue():
            if pending:
                t, a = pending.pop(0)
                rows = N_META if a[2] == META else KB
                s_ref[t % RING, :rows, :] = qk(*a)

        def read(t, u, masked):
            if u == META:
                return s_ref[t % RING, :N_META, :]
            s = s_ref[t % RING]
            return jnp.where(chunk_mask, s, NEG) if masked else s

        for t, (hh, c, u, masked) in enumerate(tiles):
            m_old, acc = state[hh, c]
            m_new = jnp.maximum(m_old, jnp.max(read(t, u, masked), axis=0, keepdims=True))
            alpha = jnp.exp2(m_old - m_new)
            p = jnp.exp2((read(t, u, masked) - m_new).astype(BF16))
            issue()
            acc = alpha * acc + jnp.dot(keys_of(hh, u, kb0)[1], p, preferred_element_type=F32)
            state[hh, c] = (m_new, acc)
        for hh, c in chains:
            cs = slice(c * KB, (c + 1) * KB)
            m_ref[hh, :, cs], acc_ref[head_rows(hh), cs] = state[hh, c]

    full_tiles = [(hh, c, u, False) for u in range(NS) for hh in range(HP) for c in range(NS)]
    diag_tiles = [(hh, c, d, c == d) for d in range(NS) for hh in range(HP) for c in range(d, NS)]
    meta_tiles = [(hh, c, META, True) for hh in range(HP) for c in range(NS)]
    assert [t[:3] for t in full_tiles[:LOOKAHEAD]] == [t[:3] for t in diag_tiles[:LOOKAHEAD]]

    def first_scores(kb0):
        return [(hh, c, u, kb0) for hh, c, u, _ in full_tiles[:LOOKAHEAD]]

    q_tile = j % TILES_PER_SEQ

    m_ref[...] = jnp.full(m_ref.shape, NEG, F32)
    acc_ref[...] = jnp.zeros(acc_ref.shape, F32)
    for t, a in enumerate(first_scores(0)):
        s_ref[t] = qk(*a)

    pair_tiles = [(hh, c, u, False) for u in range(2 * NS) for hh in range(HP) for c in range(NS)]

    def pair_body(it, carry):
        run_tiles(pair_tiles, it * 2 * NS, first_scores((it + 1) * 2 * NS))
        return carry

    lax.fori_loop(0, q_tile // 2, pair_body, 0)

    @pl.when(q_tile % 2 == 1)
    def _():
        run_tiles(full_tiles, (q_tile - 1) * NS, first_scores(q_tile * NS))

    run_tiles(diag_tiles + meta_tiles, q_tile * NS, [])

    for hh in range(HP):
        acc = acc_ref[head_rows(hh), :]
        o_ref[hh * V_HEAD:(hh + 1) * V_HEAD, :] = (
            acc[:V_HEAD, :] / acc[V_HEAD:V_HEAD + 1, :]).astype(BF16)


def _attn(qt_all, k_all, vt_all):
    def b_of(j):
        return j // TILES_PER_SEQ

    return pl.pallas_call(
        _attn_kernel,
        grid=(H // HP, NT_REAL),
        in_specs=[pl.BlockSpec((HP, LANES, BQ), lambda h, j: (h, 0, j)),
                  pl.BlockSpec((HP, SEQ, LANES), lambda h, j: (h, b_of(j), 0),
                               pipeline_mode=pl.Buffered(1)),
                  pl.BlockSpec((HP * VR, SEQ), lambda h, j: (h, b_of(j)),
                               pipeline_mode=pl.Buffered(1)),
                  pl.BlockSpec((HP, N_META, LANES), lambda h, j: (h, META_ROW0 // N_META, 0)),
                  pl.BlockSpec((HP * VR, LANES), lambda h, j: (h, META_ROW0 // LANES))],
        out_specs=pl.BlockSpec((HP * V_HEAD, BQ), lambda h, j: (h, j)),
        out_shape=jax.ShapeDtypeStruct((H * V_HEAD, B * SEQ), BF16),
        scratch_shapes=[pltpu.VMEM((HP, 1, BQ), F32), pltpu.VMEM((HP * VR, BQ), F32),
                        pltpu.VMEM((RING, KB, KB), F32)],
        compiler_params=_params(2),
        name="attn",
    )(qt_all, k_all, vt_all, k_all, vt_all)


def _rot_cols(w):
    half = QK_ROPE // 2
    return jnp.concatenate([-w[..., half:], w[..., :half]], axis=-1)


def _place_rope(w):
    pad = [(0, 0)] * (w.ndim - 1) + [(QK_NOPE, LANES - QK_NOPE - QK_ROPE)]
    return jnp.pad(w, pad)


def _rope_tables():
    inv = 1.0 / (ROPE_THETA ** (jnp.arange(0, QK_ROPE, 2, dtype=F32) / QK_ROPE))
    row = jnp.arange(TAB_ROWS)
    pos = jnp.where(row < SEQ, N_META + row, row - SEQ).astype(F32)
    ang = pos[:, None] * inv[None, :]
    ones = jnp.ones((TAB_ROWS, QK_NOPE), F32)
    ones_hi = jnp.ones((TAB_ROWS, LANES - QK_NOPE - QK_ROPE), F32)
    cos_t = jnp.concatenate([ones, jnp.cos(ang), jnp.cos(ang), ones_hi], axis=1)
    sin_t = jnp.concatenate([0 * ones, jnp.sin(ang), jnp.sin(ang), 0 * ones_hi], axis=1)
    return cos_t, sin_t


def kernel(x, meta_tokens, ffn1_norm, ffn1_w_gate, ffn1_w_up, ffn1_w_down, mix_norm, ffn2_norm, ffn2_w_gate, ffn2_w_up, ffn2_w_down, pool_w, pool_scale, kv_in_norm, w_dkv, kv_latent_norm, w_uk, w_uv, w_dq, q_latent_norm, w_uq, w_o, final_norm):
    cos_t, sin_t = _rope_tables()
    frames = x.reshape(B * SEQ, D)
    meta_tile = jnp.concatenate(
        [meta_tokens.astype(x.dtype), jnp.zeros((TM - N_META, D), x.dtype)], axis=0)
    ffn1_w = [ffn1_w_gate, ffn1_w_up, ffn1_w_down]
    ffn2_w = [ffn2_w_gate, ffn2_w_up, ffn2_w_down]

    w_kr = w_dkv[:, KV_RANK:]
    wd_kv = jnp.concatenate(
        [w_dkv[:, :KV_RANK], _place_rope(w_kr), _place_rope(_rot_cols(w_kr))], axis=1).astype(BF16)
    wuk = jnp.pad(w_uk.reshape(KV_RANK, H, QK_NOPE),
                  ((0, 0), (0, 0), (0, LANES - QK_NOPE))).reshape(KV_RANK, H * LANES).astype(BF16)
    kv_args = (kv_in_norm, wd_kv, kv_latent_norm, wuk, w_uv.T.astype(BF16), cos_t, sin_t)

    def q_args(l):
        jj = l - N_A
        wq3 = w_uq[jj].reshape(Q_RANK, H, QK_NOPE + QK_ROPE)
        wq = jnp.pad(wq3, ((0, 0), (0, 0), (0, LANES - QK_NOPE - QK_ROPE)))
        wqr = _rot_cols(wq3[:, :, QK_NOPE:])
        to_t = lambda w: w.reshape(Q_RANK, -1).T.astype(BF16)
        rope_rows = slice(QK_NOPE, QK_NOPE + QK_ROPE)
        return (mix_norm[l], w_dq[jj].astype(BF16), q_latent_norm[jj], to_t(wq), to_t(wqr),
                cos_t[:, rope_rows].T, sin_t[:, rope_rows].T)

    h = _layer(frames, ffn1_norm[0], ffn1_w, 0, pre="first", pre_args=(meta_tile,))
    h = _layer(h, ffn2_norm[0], ffn2_w, 0, pre="pool",
               pre_args=(mix_norm[0], pool_w[0].astype(BF16), pool_scale[0]))
    h = _layer(h, ffn1_norm[1], ffn1_w, 1)
    h, k_all, vt_all = _layer(h, ffn2_norm[1], ffn2_w, 1, pre="pool",
                              pre_args=(mix_norm[1], pool_w[1].astype(BF16), pool_scale[1]),
                              post="kv", post_args=kv_args)
    for l in range(N_A, DEPTH):
        h, qt_all = _layer(h, ffn1_norm[l], ffn1_w, l, post="q", post_args=q_args(l),
                           with_meta=False)
        ot = _attn(qt_all, k_all, vt_all)
        last = l == DEPTH - 1
        h = _layer(h, ffn2_norm[l], ffn2_w, l, pre="wo", pre_args=(ot, w_o[l - N_A].astype(BF16)),
                   post="final" if last else "plain", post_args=(final_norm,) if last else (),
                   with_meta=False)
    return h.reshape(B, SEQ, D)
```
